```python
import math
import jax, jax.numpy as jnp
from jax import lax
import numpy as np

D_MODEL = 1024
BATCH = 16
SEQ = 2048
DEPTH = 2
DEC_BATCH = 4
DEC_SEQ = 4096
PAST_LEN = 128

GRID_W = 64
HEAD_DIM = 64
NA_HEADS = 8
NA_WIN_H = 8
NA_WIN_W = 16
GQA_HEADS = 8
GQA_KV_HEADS = 2
GQA_BLOCK = 128
ROPE_THETA = 10000.0
POOL_WINDOWS = (2, 4, 8, 16)
POOL_GROUPS = 4
POOL_CH = D_MODEL // POOL_GROUPS
N_MEM = 256
XA_HEADS = 4
XA_HEAD_DIM = D_MODEL // XA_HEADS
D_FF = 2816
CONV_W = 3
LN_EPS = 1e-5
QK_EPS = 1e-6
NEG_INF = -1e30
DN_ALPHA = (2 * DEPTH) ** 0.25
DN_BETA = (8 * DEPTH) ** -0.25
N_EVEN = (DEPTH + 1) // 2
N_ODD = DEPTH // 2
NA_WIDTH = NA_HEADS * HEAD_DIM
GQA_Q_WIDTH = GQA_HEADS * HEAD_DIM
GQA_KV_WIDTH = GQA_KV_HEADS * HEAD_DIM
AB_IN = 3 * NA_WIDTH + GQA_Q_WIDTH + 2 * GQA_KV_WIDTH
AB_OUT = NA_WIDTH + GQA_Q_WIDTH
AB_SPLITS = (NA_WIDTH, 2 * NA_WIDTH, 3 * NA_WIDTH, 3 * NA_WIDTH + GQA_Q_WIDTH,
             3 * NA_WIDTH + GQA_Q_WIDTH + GQA_KV_WIDTH)

kernel_name = "hybrid_natten_gqa_pool_encoder"


def layer_norm(x, g, b):
    xf = x.astype(jnp.float32)
    mu = jnp.mean(xf, axis=-1, keepdims=True)
    xc = xf - mu
    var = jnp.mean(xc * xc, axis=-1, keepdims=True)
    return (xc * lax.rsqrt(var + LN_EPS) * g + b).astype(x.dtype)


def rms_norm_heads(x, g):
    xf = x.astype(jnp.float32)
    return xf * lax.rsqrt(jnp.mean(xf * xf, axis=-1, keepdims=True) + QK_EPS) * g


def rope_1d(x, pos):
    half = x.shape[-1] // 2
    inv_freq = ROPE_THETA ** (-jnp.arange(half, dtype=jnp.float32) / half)
    ang = pos[:, None] * inv_freq[None, :]
    cos = jnp.cos(ang)[None, :, None, :]
    sin = jnp.sin(ang)[None, :, None, :]
    x1, x2 = x[..., :half], x[..., half:]
    return jnp.concatenate([x1 * cos - x2 * sin, x2 * cos + x1 * sin], axis=-1)


def axial_rope(x):
    S = x.shape[1]
    t = jnp.arange(S)
    row = (t // GRID_W).astype(jnp.float32)
    col = (t % GRID_W).astype(jnp.float32)
    half = HEAD_DIM // 2
    return jnp.concatenate([rope_1d(x[..., :half], row), rope_1d(x[..., half:], col)], axis=-1)


def neighbourhood_attention(q, k, v, rpb):
    B, S, H, d = q.shape
    R = S // GRID_W
    KH = min(NA_WIN_H, R)
    KW = NA_WIN_W
    r = np.arange(R)
    rs = np.clip(r - KH // 2, 0, R - KH)
    rows_idx = rs[:, None] + np.arange(KH)[None, :]
    dr_idx = rows_idx - r[:, None] + (NA_WIN_H - 1)
    c = np.arange(GRID_W)
    cs = np.clip(c - KW // 2, 0, GRID_W - KW)
    col_valid = (c[None, :] >= cs[:, None]) & (c[None, :] < cs[:, None] + KW)
    dc_idx = np.clip(c[None, :] - c[:, None] + (KW - 1), 0, 2 * KW - 2)
    bias = rpb[:, dr_idx[:, None, :, None], dc_idx[None, :, None, :]].astype(jnp.float32)
    bias = jnp.where(col_valid[None, None, :, None, :], bias, NEG_INF)
    qg = q.reshape(B, R, GRID_W, H, d)
    kg = k.reshape(B, R, GRID_W, H, d)[:, rows_idx]
    vg = v.reshape(B, R, GRID_W, H, d)[:, rows_idx]
    s = jnp.einsum('brqhd,brkwhd->bhrqkw', qg, kg,
                   preferred_element_type=jnp.float32) * (d ** -0.5) + bias[None]
    p = jax.nn.softmax(s.reshape(B, H, R, GRID_W, KH * GRID_W), axis=-1)
    p = p.reshape(B, H, R, GRID_W, KH, GRID_W).astype(v.dtype)
    o = jnp.einsum('bhrqkw,brkwhd->brqhd', p, vg)
    return o.reshape(B, S, H * d)


def gqa_axial_attention(q, k, v, q_gain, k_gain):
    B, S = q.shape[:2]
    G = GQA_HEADS // GQA_KV_HEADS
    q = axial_rope(rms_norm_heads(q, q_gain)).astype(v.dtype)
    k = axial_rope(rms_norm_heads(k, k_gain)).astype(v.dtype)
    nblk = S // GQA_BLOCK
    qb = q.reshape(B, nblk, GQA_BLOCK, GQA_KV_HEADS, G, HEAD_DIM).swapaxes(0, 1)
    scale = HEAD_DIM ** -0.5

    def attend(qblk):
        s = jnp.einsum('bqkgd,bskd->bkgqs', qblk, k, preferred_element_type=jnp.float32) * scale
        p = jax.nn.softmax(s, axis=-1).astype(v.dtype)
        return jnp.einsum('bkgqs,bskd->bqkgd', p, v)

    o = lax.map(attend, qb)
    return o.swapaxes(0, 1).reshape(B, S, GQA_HEADS * HEAD_DIM)


def mixer_ab(x, w_in, rpb, q_gain, k_gain, w_out):
    B, S, _ = x.shape
    h = x @ w_in
    qa, ka, va, qb, kb, vb = jnp.split(h, AB_SPLITS, axis=-1)
    heads = lambda t, n: t.reshape(B, S, n, HEAD_DIM)
    oa = neighbourhood_attention(heads(qa, NA_HEADS), heads(ka, NA_HEADS), heads(va, NA_HEADS), rpb)
    ob = gqa_axial_attention(heads(qb, GQA_HEADS), heads(kb, GQA_KV_HEADS),
                             heads(vb, GQA_KV_HEADS), q_gain, k_gain)
    return jnp.concatenate([oa, ob], axis=-1) @ w_out


def pool_mixer(x, w, b, scale):
    B, S, D = x.shape
    xg = x.reshape(B, S, POOL_GROUPS, POOL_CH)
    t = jnp.arange(S)
    outs = []
    for gi, win in enumerate(POOL_WINDOWS):
        xi = xg[:, :, gi, :].astype(jnp.float32)
        csum = jnp.concatenate([jnp.zeros((B, 1, POOL_CH), jnp.float32), jnp.cumsum(xi, axis=1)], axis=1)
        lo = jnp.clip(t - win // 2, 0, S - 1)
        hi = jnp.clip(t + (win - 1 - win // 2), 0, S - 1)
        cnt = (hi - lo + 1).astype(jnp.float32)[None, :, None]
        mean = (jnp.take(csum, hi + 1, axis=1) - jnp.take(csum, lo, axis=1)) / cnt
        outs.append(mean - xi)
    pooled = jnp.stack(outs, axis=2).astype(x.dtype)
    y = jnp.einsum('bsgc,gcd->bsgd', pooled, w) + b
    return y.reshape(B, S, D) * scale


def mem_cross_attention(x, mem, wq, wkv, wo):
    B, S, D = x.shape
    M = mem.shape[1]
    q = (x @ wq).reshape(B, S, XA_HEADS, XA_HEAD_DIM)
    kv = (mem @ wkv).reshape(B, M, 2, XA_HEADS, XA_HEAD_DIM)
    s = jnp.einsum('bshd,bmhd->bhsm', q, kv[:, :, 0],
                   preferred_element_type=jnp.float32) * (XA_HEAD_DIM ** -0.5)
    p = jax.nn.softmax(s, axis=-1).astype(x.dtype)
    o = jnp.einsum('bhsm,bmhd->bshd', p, kv[:, :, 1]).reshape(B, S, D)
    return o @ wo


def conv_ffn(x, w_up, conv_w, conv_b, w_down):
    h = x @ w_up
    hp = jnp.pad(h, ((0, 0), (1, 1), (0, 0)))
    h = hp[:, :-2] * conv_w[0] + hp[:, 1:-1] * conv_w[1] + hp[:, 2:] * conv_w[2] + conv_b
    val, gate = jnp.split(h, 2, axis=-1)
    return (jax.nn.gelu(gate) * val) @ w_down


def run_trunk(x, mem, ab_w_in, na_rpb, gqa_q_gain, gqa_k_gain, ab_w_out, pool_w, pool_b,
              pool_scale, ln1_g, ln1_b, xa_wq, xa_wkv, xa_wo, ln2_g, ln2_b, ffn_w_up,
              ffn_conv_w, ffn_conv_b, ffn_w_down, ln3_g, ln3_b):
    for l in range(DEPTH):
        i = l // 2
        if l % 2 == 0:
            m = mixer_ab(x, ab_w_in[i], na_rpb[i], gqa_q_gain[i], gqa_k_gain[i], ab_w_out[i])
        else:
            m = pool_mixer(x, pool_w[i], pool_b[i], pool_scale[i])
        x = layer_norm(DN_ALPHA * x + m, ln1_g[l], ln1_b[l])
        x = layer_norm(DN_ALPHA * x + mem_cross_attention(x, mem, xa_wq[l], xa_wkv[l], xa_wo[l]),
                       ln2_g[l], ln2_b[l])
        x = layer_norm(DN_ALPHA * x + conv_ffn(x, ffn_w_up[l], ffn_conv_w[l], ffn_conv_b[l], ffn_w_down[l]),
                       ln3_g[l], ln3_b[l])
    return x


def setup_inputs(seed: int = 0) -> dict:
    key = jax.random.key(seed)
    ks = jax.random.split(key, 26)
    f32 = jnp.float32
    nrm = lambda k, shape, s: jax.random.normal(k, shape, f32) * s
    D = D_MODEL
    return {
        "x_prompt": nrm(ks[0], (BATCH, SEQ, D), 1.0),
        "x_sample": nrm(ks[1], (DEC_BATCH, DEC_SEQ, D), 1.0),
        "mem_prompt": nrm(ks[2], (BATCH, N_MEM, D), 1.0),
        "mem_sample": nrm(ks[3], (DEC_BATCH, N_MEM, D), 1.0),
        "ab_w_in": nrm(ks[4], (N_EVEN, D, AB_IN), D ** -0.5),
        "na_rpb": nrm(ks[5], (N_EVEN, NA_HEADS, 2 * NA_WIN_H - 1, 2 * NA_WIN_W - 1), 0.1),
        "gqa_q_gain": 1.0 + nrm(ks[6], (N_EVEN, HEAD_DIM), 0.02),
        "gqa_k_gain": 1.0 + nrm(ks[7], (N_EVEN, HEAD_DIM), 0.02),
        "ab_w_out": nrm(ks[8], (N_EVEN, AB_OUT, D), AB_OUT ** -0.5 * DN_BETA),
        "pool_w": nrm(ks[9], (N_ODD, POOL_GROUPS, POOL_CH, POOL_CH), POOL_CH ** -0.5 * DN_BETA),
        "pool_b": nrm(ks[10], (N_ODD, POOL_GROUPS, POOL_CH), 0.01),
        "pool_scale": 1.0 + nrm(ks[11], (N_ODD, D), 0.02),
        "ln1_g": 1.0 + nrm(ks[12], (DEPTH, D), 0.02),
        "ln1_b": nrm(ks[13], (DEPTH, D), 0.02),
        "xa_wq": nrm(ks[14], (DEPTH, D, D), D ** -0.5),
        "xa_wkv": nrm(ks[15], (DEPTH, D, 2 * D), D ** -0.5),
        "xa_wo": nrm(ks[16], (DEPTH, D, D), D ** -0.5 * DN_BETA),
        "ln2_g": 1.0 + nrm(ks[17], (DEPTH, D), 0.02),
        "ln2_b": nrm(ks[18], (DEPTH, D), 0.02),
        "ffn_w_up": nrm(ks[19], (DEPTH, D, 2 * D_FF), D ** -0.5),
        "ffn_conv_w": nrm(ks[20], (DEPTH, CONV_W, 2 * D_FF), CONV_W ** -0.5),
        "ffn_conv_b": nrm(ks[21], (DEPTH, 2 * D_FF), 0.02),
        "ffn_w_down": nrm(ks[22], (DEPTH, D_FF, D), D_FF ** -0.5 * DN_BETA),
        "ln3_g": 1.0 + nrm(ks[23], (DEPTH, D), 0.02),
        "ln3_b": nrm(ks[24], (DEPTH, D), 0.02),
    }


def reference(x_prompt, x_sample, mem_prompt, mem_sample, ab_w_in, na_rpb, gqa_q_gain, gqa_k_gain,
              ab_w_out, pool_w, pool_b, pool_scale, ln1_g, ln1_b, xa_wq, xa_wkv, xa_wo, ln2_g, ln2_b,
              ffn_w_up, ffn_conv_w, ffn_conv_b, ffn_w_down, ln3_g, ln3_b):
    y_prompt = run_trunk(x_prompt, mem_prompt, ab_w_in, na_rpb, gqa_q_gain, gqa_k_gain, ab_w_out,
                         pool_w, pool_b, pool_scale, ln1_g, ln1_b, xa_wq, xa_wkv, xa_wo, ln2_g, ln2_b,
                         ffn_w_up, ffn_conv_w, ffn_conv_b, ffn_w_down, ln3_g, ln3_b)
    y_sample = run_trunk(x_sample, mem_sample, ab_w_in, na_rpb, gqa_q_gain, gqa_k_gain, ab_w_out,
                         pool_w, pool_b, pool_scale, ln1_g, ln1_b, xa_wq, xa_wkv, xa_wo, ln2_g, ln2_b,
                         ffn_w_up, ffn_conv_w, ffn_conv_b, ffn_w_down, ln3_g, ln3_b)
    return (y_prompt, y_sample)
```

```python
import functools

import jax
import jax.numpy as jnp
from jax import lax
from jax.experimental import pallas as pl
from jax.experimental.pallas import tpu as pltpu

F32 = jnp.float32
BF16 = jnp.bfloat16

D_MODEL = 1024
DEPTH = 2
GRID_W = 64
HEAD_DIM = 64
NA_HEADS = 8
NA_WIN_H = 8
NA_WIN_W = 16
GQA_HEADS = 8
GQA_KV_HEADS = 2
ROPE_THETA = 10000.0
POOL_WINDOWS = (2, 4, 8, 16)
POOL_GROUPS = 4
POOL_CH = D_MODEL // POOL_GROUPS
XA_HEADS = 4
XA_HEAD_DIM = D_MODEL // XA_HEADS
D_FF = 2816
LN_EPS = 1e-5
QK_EPS = 1e-6
NEG_INF = -1e30
DN_ALPHA = (2 * DEPTH) ** 0.25
NA_WIDTH = NA_HEADS * HEAD_DIM
GQA_Q_WIDTH = GQA_HEADS * HEAD_DIM
GQA_KV_WIDTH = GQA_KV_HEADS * HEAD_DIM
AB_IN = 3 * NA_WIDTH + GQA_Q_WIDTH + 2 * GQA_KV_WIDTH
ATTN_SCALE = HEAD_DIM ** -0.5
XA_SCALE = XA_HEAD_DIM ** -0.5

LANES = 128
BF16_ROWS = 16
F32_ROWS = 8
TOKEN_TILE = 512
GQA_Q_TILE = 128
FFN_CHUNK = 256
FFN_NCHUNK = D_FF // FFN_CHUNK
NA_KEYS = NA_WIN_H * GRID_W
VMEM_LIMIT_BYTES = 56 * 1024 * 1024


def _params(n_axes):
    return pltpu.CompilerParams(
        dimension_semantics=("arbitrary",) * n_axes,
        vmem_limit_bytes=VMEM_LIMIT_BYTES)


def _resident(shape):
    nd = len(shape)
    return pl.BlockSpec(shape, lambda *_: (0,) * nd, pipeline_mode=pl.Buffered(1))


def _dot(a, b):
    return jnp.dot(a, b, preferred_element_type=F32)


def _dot_nt(a, b):
    return lax.dot_general(a, b, (((1,), (1,)), ((), ())), preferred_element_type=F32)


def _layer_norm(z, g, b):
    mu = jnp.mean(z, axis=-1, keepdims=True)
    zc = z - mu
    var = jnp.mean(zc * zc, axis=-1, keepdims=True)
    return zc * lax.rsqrt(var + LN_EPS) * g + b


def _softmax_parts(s):
    m = jnp.max(s, axis=-1, keepdims=True)
    e = jnp.exp(s - m)
    return e, jnp.sum(e, axis=-1, keepdims=True)


def _rope_partner(x):
    lane = lax.broadcasted_iota(jnp.int32, x.shape, 1)
    first_half = (lane & 16) == 0
    return jnp.where(first_half, pltpu.roll(x, LANES - 16, 1), pltpu.roll(x, 16, 1))


def _norm_rope(x, seg_ones, gain, cos, sin):
    x2 = x * x
    hi = x2.astype(BF16)
    lo = (x2 - hi.astype(F32)).astype(BF16)
    ss = _dot(hi, seg_ones) + _dot(lo, seg_ones)
    xn = x * lax.rsqrt(ss * (1.0 / HEAD_DIM) + QK_EPS) * gain
    return xn * cos + _rope_partner(xn) * sin


def _proj_ab_kernel(x_ref, w_ref, cos_ref, sin_ref, ones_ref, qg_ref, kg_ref,
                    qkva_ref, qb_ref, kvb_ref):
    xb = x_ref[...].astype(BF16)
    ha = _dot(xb, w_ref[:, :3 * NA_WIDTH])
    qkva_ref[:, :NA_WIDTH] = (ha[:, :NA_WIDTH] * ATTN_SCALE).astype(BF16)
    qkva_ref[:, NA_WIDTH:] = ha[:, NA_WIDTH:].astype(BF16)
    hb = _dot(xb, w_ref[:, 3 * NA_WIDTH:])
    cos = cos_ref[...]
    sin = sin_ref[...]
    ones = ones_ref[...]
    for c in range(GQA_Q_WIDTH // LANES):
        sl = slice(c * LANES, (c + 1) * LANES)
        q = _norm_rope(hb[:, sl], ones, qg_ref[...], cos, sin)
        qb_ref[:, sl] = (q * ATTN_SCALE).astype(BF16)
    k0 = GQA_Q_WIDTH
    kvb_ref[:, :GQA_KV_WIDTH] = _norm_rope(
        hb[:, k0:k0 + GQA_KV_WIDTH], ones, kg_ref[...], cos, sin).astype(BF16)
    kvb_ref[:, GQA_KV_WIDTH:] = hb[:, k0 + GQA_KV_WIDTH:].astype(BF16)


def _rope_tables(seq):
    t = jnp.arange(seq)
    row = (t // GRID_W).astype(F32)
    col = (t % GRID_W).astype(F32)
    quarter = HEAD_DIM // 4
    inv_freq = ROPE_THETA ** (-jnp.arange(quarter, dtype=F32) / quarter)
    ar = row[:, None] * inv_freq[None, :]
    ac = col[:, None] * inv_freq[None, :]
    cos = jnp.concatenate([jnp.cos(ar), jnp.cos(ar), jnp.cos(ac), jnp.cos(ac)], axis=-1)
    sin = jnp.concatenate([-jnp.sin(ar), jnp.sin(ar), -jnp.sin(ac), jnp.sin(ac)], axis=-1)
    reps = LANES // HEAD_DIM
    return jnp.tile(cos, (1, reps)), jnp.tile(sin, (1, reps))


def _proj_ab(x2d, seq, w_in, q_gain, k_gain):
    tokens = x2d.shape[0]
    tm = TOKEN_TILE
    tiles_per_seq = seq // tm
    cos, sin = _rope_tables(seq)
    seg = jnp.arange(LANES) // HEAD_DIM
    seg_ones = (seg[:, None] == seg[None, :]).astype(BF16)
    reps = LANES // HEAD_DIM
    qg = jnp.tile(q_gain, reps)[None, :]
    kg = jnp.tile(k_gain, reps)[None, :]
    tile = lambda width: pl.BlockSpec((tm, width), lambda i: (i, 0))
    table = pl.BlockSpec((tm, LANES), lambda i: (i % tiles_per_seq, 0))
    return pl.pallas_call(
        _proj_ab_kernel,
        grid=(tokens // tm,),
        in_specs=[tile(D_MODEL), _resident((D_MODEL, AB_IN)), table, table,
                  _resident((LANES, LANES)), _resident((1, LANES)), _resident((1, LANES))],
        out_specs=[tile(3 * NA_WIDTH), tile(GQA_Q_WIDTH), tile(2 * GQA_KV_WIDTH)],
        out_shape=[jax.ShapeDtypeStruct((tokens, 3 * NA_WIDTH), BF16),
                   jax.ShapeDtypeStruct((tokens, GQA_Q_WIDTH), BF16),
                   jax.ShapeDtypeStruct((tokens, 2 * GQA_KV_WIDTH), BF16)],
        compiler_params=_params(1),
        name="proj_ab",
    )(x2d, w_in, cos, sin, seg_ones, qg, kg)


def _na_bias_table(rpb):
    off = jnp.arange(NA_WIN_H)
    kh = jnp.arange(NA_WIN_H)
    dr = kh[None, :] - off[:, None] + (NA_WIN_H - 1)
    c = jnp.arange(GRID_W)
    cs = jnp.clip(c - NA_WIN_W // 2, 0, GRID_W - NA_WIN_W)
    col_valid = (c[None, :] >= cs[:, None]) & (c[None, :] < cs[:, None] + NA_WIN_W)
    dc = jnp.clip(c[None, :] - c[:, None] + (NA_WIN_W - 1), 0, 2 * NA_WIN_W - 2)
    bias = rpb[:, dr[:, None, :, None], dc[None, :, None, :]].astype(F32)
    bias = jnp.where(col_valid[None, None, :, None, :], bias, NEG_INF)
    bias = jnp.transpose(bias, (1, 0, 2, 3, 4))
    return bias.reshape(NA_WIN_H, NA_HEADS, GRID_W, NA_KEYS)


def _na_window_start(r, n_rows):
    return jnp.clip(r - NA_WIN_H // 2, 0, n_rows - NA_WIN_H)


def _na_kernel(q_ref, k_ref, v_ref, bias_ref, o_ref, *, n_rows):
    r = pl.program_id(1)
    start = pl.multiple_of(_na_window_start(r, n_rows) * GRID_W, GRID_W)
    lower = lax.broadcasted_iota(jnp.int32, (GRID_W, LANES), 1) < HEAD_DIM
    for p in range(NA_HEADS // 2):
        sl = slice(p * LANES, (p + 1) * LANES)
        qp = q_ref[:, sl]
        zero = jnp.zeros_like(qp)
        lhs = jnp.concatenate([jnp.where(lower, qp, zero), jnp.where(lower, zero, qp)], axis=0)
        k = k_ref[pl.ds(start, NA_KEYS), sl]
        v = v_ref[pl.ds(start, NA_KEYS), sl]
        bias = jnp.concatenate([bias_ref[2 * p], bias_ref[2 * p + 1]], axis=0)
        e, l = _softmax_parts(_dot_nt(lhs, k) + bias)
        o = _dot(e.astype(BF16), v) / l
        o_ref[:, sl] = jnp.where(lower, o[:GRID_W], o[GRID_W:]).astype(BF16)


def _na_attn(qkva, batch, seq, bias_table):
    n_rows = seq // GRID_W
    assert n_rows >= NA_WIN_H
    row_off = lambda r: r - _na_window_start(r, n_rows)
    return pl.pallas_call(
        functools.partial(_na_kernel, n_rows=n_rows),
        grid=(batch, n_rows),
        in_specs=[
            pl.BlockSpec((GRID_W, NA_WIDTH), lambda b, r: (b * n_rows + r, 0)),
            pl.BlockSpec((seq, NA_WIDTH), lambda b, r: (b, 1)),
            pl.BlockSpec((seq, NA_WIDTH), lambda b, r: (b, 2)),
            pl.BlockSpec((None, NA_HEADS, GRID_W, NA_KEYS), lambda b, r: (row_off(r), 0, 0, 0)),
        ],
        out_specs=pl.BlockSpec((GRID_W, NA_WIDTH), lambda b, r: (b * n_rows + r, 0)),
        out_shape=jax.ShapeDtypeStruct((batch * seq, NA_WIDTH), BF16),
        compiler_params=_params(2),
        name="na_attn",
    )(qkva, qkva, qkva, bias_table)


def _gqa_kernel(q_ref, k_ref, v_ref, o_ref):
    tq = q_ref.shape[0]
    lower = lax.broadcasted_iota(jnp.int32, (tq, LANES), 1) < HEAD_DIM
    k = k_ref[...]
    v = v_ref[...]
    heads_per_kv = GQA_HEADS // GQA_KV_HEADS
    for p in range(GQA_HEADS // 2):
        sl = slice(p * LANES, (p + 1) * LANES)
        kv_head = (2 * p) // heads_per_kv
        qp = q_ref[:, sl]
        qr = pltpu.roll(qp, HEAD_DIM, 1)
        zero = jnp.zeros_like(qp)
        if kv_head == 0:
            lhs = jnp.concatenate([jnp.where(lower, qp, zero), jnp.where(lower, qr, zero)], axis=0)
        else:
            lhs = jnp.concatenate([jnp.where(lower, zero, qr), jnp.where(lower, zero, qp)], axis=0)
        e, l = _softmax_parts(_dot_nt(lhs, k))
        o = _dot(e.astype(BF16), v) / l
        o_even, o_odd = o[:tq], o[tq:]
        if kv_head == 0:
            out = jnp.where(lower, o_even, pltpu.roll(o_odd, HEAD_DIM, 1))
        else:
            out = jnp.where(lower, pltpu.roll(o_even, HEAD_DIM, 1), o_odd)
        o_ref[:, sl] = out.astype(BF16)


def _gqa_attn(qb, kvb, batch, seq):
    tq = GQA_Q_TILE
    n_q = seq // tq
    return pl.pallas_call(
        _gqa_kernel,
        grid=(batch, n_q),
        in_specs=[
            pl.BlockSpec((tq, GQA_Q_WIDTH), lambda b, i: (b * n_q + i, 0)),
            pl.BlockSpec((seq, GQA_KV_WIDTH), lambda b, i: (b, 0)),
            pl.BlockSpec((seq, GQA_KV_WIDTH), lambda b, i: (b, 1)),
        ],
        out_specs=pl.BlockSpec((tq, GQA_Q_WIDTH), lambda b, i: (b * n_q + i, 0)),
        out_shape=jax.ShapeDtypeStruct((batch * seq, GQA_Q_WIDTH), BF16),
        compiler_params=_params(2),
        name="gqa_attn",
    )(qb, kvb, kvb)


def _out_ln_kernel(x_ref, oa_ref, ob_ref, w_ref, g_ref, b_ref, y_ref):
    m = _dot(oa_ref[...], w_ref[:NA_WIDTH, :]) + _dot(ob_ref[...], w_ref[NA_WIDTH:, :])
    y_ref[...] = _layer_norm(DN_ALPHA * x_ref[...] + m, g_ref[...], b_ref[...])


def _out_ln(x2d, oa, ob, w_out, g, b):
    tokens = x2d.shape[0]
    tm = TOKEN_TILE
    tile = lambda width: pl.BlockSpec((tm, width), lambda i: (i, 0))
    return pl.pallas_call(
        _out_ln_kernel,
        grid=(tokens // tm,),
        in_specs=[tile(D_MODEL), tile(NA_WIDTH), tile(GQA_Q_WIDTH),
                  _resident((NA_WIDTH + GQA_Q_WIDTH, D_MODEL)),
                  _resident((1, D_MODEL)), _resident((1, D_MODEL))],
        out_specs=tile(D_MODEL),
        out_shape=jax.ShapeDtypeStruct((tokens, D_MODEL), F32),
        compiler_params=_params(1),
        name="out_ln",
    )(x2d, oa, ob, w_out, g[None, :], b[None, :])


POOL_HALO = F32_ROWS


def _pool_ln_kernel(xp_ref, x_ref, xn_ref, w_ref, pb_ref, ps_ref, g_ref, b_ref, y_ref, xe_ref,
                    *, seq):
    i = pl.program_id(1)
    tm = x_ref.shape[0]
    x = x_ref[...]
    xe_ref[:POOL_HALO, :] = jnp.where(i > 0, xp_ref[...], 0.0)
    xe_ref[POOL_HALO:POOL_HALO + tm, :] = x
    xe_ref[POOL_HALO + tm:, :] = jnp.where(i < pl.num_programs(1) - 1, xn_ref[...], 0.0)
    pos = i * tm + lax.broadcasted_iota(jnp.int32, (tm, 1), 0)
    parts = []
    for gi, win in enumerate(POOL_WINDOWS):
        cols = slice(gi * POOL_CH, (gi + 1) * POOL_CH)
        lo, hi = -(win // 2), win - 1 - win // 2
        total = xe_ref[POOL_HALO + lo:POOL_HALO + lo + tm, cols]
        for d in range(lo + 1, hi + 1):
            total = total + xe_ref[POOL_HALO + d:POOL_HALO + d + tm, cols]
        cnt = jnp.minimum(pos + hi, seq - 1) - jnp.maximum(pos + lo, 0) + 1
        xi = x[:, cols]
        pooled = (total / cnt.astype(F32) - xi).astype(BF16)
        y = (_dot(pooled, w_ref[gi]) + pb_ref[:, cols]) * ps_ref[:, cols]
        parts.append(DN_ALPHA * xi + y)
    y_ref[...] = _layer_norm(jnp.concatenate(parts, axis=-1), g_ref[...], b_ref[...])


def _pool_ln(x2d, batch, seq, w, pb, ps, g, b):
    tm = TOKEN_TILE
    n_t = seq // tm
    halo_per_tile = tm // POOL_HALO
    n_halo = batch * seq // POOL_HALO
    prev_idx = lambda bb, i: (jnp.maximum((bb * n_t + i) * halo_per_tile - 1, 0), 0)
    next_idx = lambda bb, i: (jnp.minimum((bb * n_t + i + 1) * halo_per_tile, n_halo - 1), 0)
    return pl.pallas_call(
        functools.partial(_pool_ln_kernel, seq=seq),
        grid=(batch, n_t),
        in_specs=[
            pl.BlockSpec((POOL_HALO, D_MODEL), prev_idx),
            pl.BlockSpec((tm, D_MODEL), lambda bb, i: (bb * n_t + i, 0)),
            pl.BlockSpec((POOL_HALO, D_MODEL), next_idx),
            _resident((POOL_GROUPS, POOL_CH, POOL_CH)),
            _resident((1, D_MODEL)), _resident((1, D_MODEL)),
            _resident((1, D_MODEL)), _resident((1, D_MODEL)),
        ],
        out_specs=pl.BlockSpec((tm, D_MODEL), lambda bb, i: (bb * n_t + i, 0)),
        out_shape=jax.ShapeDtypeStruct((batch * seq, D_MODEL), F32),
        scratch_shapes=[pltpu.VMEM((tm + 2 * POOL_HALO, D_MODEL), F32)],
        compiler_params=_params(2),
        name="pool_ln",
    )(x2d, x2d, x2d, w, pb.reshape(1, D_MODEL), ps[None, :], g[None, :], b[None, :])


def _kv_proj_kernel(mem_ref, w_ref, kt_ref, v_ref):
    kv = _dot(mem_ref[...].astype(BF16), w_ref[...])
    kt_ref[...] = kv[:, :D_MODEL].T.astype(BF16)
    v_ref[...] = kv[:, D_MODEL:].astype(BF16)


def _kv_proj(mem2d, batch, n_mem, wkv):
    return pl.pallas_call(
        _kv_proj_kernel,
        grid=(batch,),
        in_specs=[pl.BlockSpec((n_mem, D_MODEL), lambda b: (b, 0)),
                  _resident((D_MODEL, 2 * D_MODEL))],
        out_specs=[pl.BlockSpec((None, D_MODEL, n_mem), lambda b: (b, 0, 0)),
                   pl.BlockSpec((None, n_mem, D_MODEL), lambda b: (b, 0, 0))],
        out_shape=[jax.ShapeDtypeStruct((batch, D_MODEL, n_mem), BF16),
                   jax.ShapeDtypeStruct((batch, n_mem, D_MODEL), BF16)],
        compiler_params=_params(1),
        name="kv_proj",
    )(mem2d, wkv)


def _xattn_ln_kernel(x_ref, kt_ref, v_ref, wq_ref, wo_ref, g_ref, b_ref, y_ref):
    x = x_ref[...]
    q = (_dot(x.astype(BF16), wq_ref[...]) * XA_SCALE).astype(BF16)
    outs = []
    for h in range(XA_HEADS):
        sl = slice(h * XA_HEAD_DIM, (h + 1) * XA_HEAD_DIM)
        e, l = _softmax_parts(_dot(q[:, sl], kt_ref[sl, :]))
        outs.append((_dot(e.astype(BF16), v_ref[:, sl]) / l).astype(BF16))
    m = _dot(jnp.concatenate(outs, axis=-1), wo_ref[...])
    y_ref[...] = _layer_norm(DN_ALPHA * x + m, g_ref[...], b_ref[...])


def _xattn_ln(x2d, batch, seq, kt, v, wq, wo, g, b):
    tm = TOKEN_TILE
    n_t = seq // tm
    n_mem = v.shape[1]
    return pl.pallas_call(
        _xattn_ln_kernel,
        grid=(batch, n_t),
        in_specs=[
            pl.BlockSpec((tm, D_MODEL), lambda bb, i: (bb * n_t + i, 0)),
            pl.BlockSpec((None, D_MODEL, n_mem), lambda bb, i: (bb, 0, 0)),
            pl.BlockSpec((None, n_mem, D_MODEL), lambda bb, i: (bb, 0, 0)),
            _resident((D_MODEL, D_MODEL)), _resident((D_MODEL, D_MODEL)),
            _resident((1, D_MODEL)), _resident((1, D_MODEL)),
        ],
        out_specs=pl.BlockSpec((tm, D_MODEL), lambda bb, i: (bb * n_t + i, 0)),
        out_shape=jax.ShapeDtypeStruct((batch * seq, D_MODEL), F32),
        compiler_params=_params(2),
        name="xattn_ln",
    )(x2d, kt, v, wq, wo, g[None, :], b[None, :])


FFN_HALO = BF16_ROWS


def _gelu_tanh(x):
    c = 0.7978845608028654
    return 0.5 * x * (1.0 + jnp.tanh(c * (x + 0.044715 * (x * x * x))))


def _ffn_ln_kernel(xp_ref, x_ref, xn_ref, wup_ref, cw_ref, cb_ref, wdn_ref, g_ref, b_ref, y_ref,
                   xe_ref, h_ref, act_ref):
    i = pl.program_id(1)
    tm = x_ref.shape[0]
    x = x_ref[...]
    xe_ref[:FFN_HALO, :] = jnp.where(i > 0, xp_ref[...], 0.0).astype(BF16)
    xe_ref[FFN_HALO:FFN_HALO + tm, :] = x.astype(BF16)
    xe_ref[FFN_HALO + tm:, :] = jnp.where(i < pl.num_programs(1) - 1, xn_ref[...], 0.0).astype(BF16)
    for c in range(FFN_NCHUNK):
        hbuf = h_ref.at[c % 2]
        hbuf[...] = _dot(xe_ref[...], wup_ref[c])
        cw = cw_ref[c]
        hc = (hbuf[FFN_HALO - 1:FFN_HALO - 1 + tm, :] * cw[0:1]
              + hbuf[FFN_HALO:FFN_HALO + tm, :] * cw[1:2]
              + hbuf[FFN_HALO + 1:FFN_HALO + 1 + tm, :] * cw[2:3]
              + cb_ref[c])
        act = _gelu_tanh(hc[:, FFN_CHUNK:]) * hc[:, :FFN_CHUNK]
        act_ref[:, c * FFN_CHUNK:(c + 1) * FFN_CHUNK] = act.astype(BF16)
    m = _dot(act_ref[...], wdn_ref[...])
    y_ref[...] = _layer_norm(DN_ALPHA * x + m, g_ref[...], b_ref[...])


def _ffn_weights(w_up, conv_w, conv_b):
    def pair(a):
        lead = a.shape[:-1]
        a = a.reshape(lead + (2, FFN_NCHUNK, FFN_CHUNK))
        a = jnp.moveaxis(a, -3, -2)
        a = a.reshape(lead + (FFN_NCHUNK, 2 * FFN_CHUNK))
        return jnp.moveaxis(a, -2, 0)
    return pair(w_up).astype(BF16), pair(conv_w), pair(conv_b[None, :])


def _ffn_ln(x2d, batch, seq, wup, cw, cb, wdn, g, b):
    tm = TOKEN_TILE
    n_t = seq // tm
    halo_per_tile = tm // FFN_HALO
    n_halo = batch * seq // FFN_HALO
    prev_idx = lambda bb, i: (jnp.maximum((bb * n_t + i) * halo_per_tile - 1, 0), 0)
    next_idx = lambda bb, i: (jnp.minimum((bb * n_t + i + 1) * halo_per_tile, n_halo - 1), 0)
    return pl.pallas_call(
        _ffn_ln_kernel,
        grid=(batch, n_t),
        in_specs=[
            pl.BlockSpec((FFN_HALO, D_MODEL), prev_idx),
            pl.BlockSpec((tm, D_MODEL), lambda bb, i: (bb * n_t + i, 0)),
            pl.BlockSpec((FFN_HALO, D_MODEL), next_idx),
            _resident((FFN_NCHUNK, D_MODEL, 2 * FFN_CHUNK)),
            _resident((FFN_NCHUNK, 3, 2 * FFN_CHUNK)),
            _resident((FFN_NCHUNK, 1, 2 * FFN_CHUNK)),
            _resident((D_FF, D_MODEL)),
            _resident((1, D_MODEL)), _resident((1, D_MODEL)),
        ],
        out_specs=pl.BlockSpec((tm, D_MODEL), lambda bb, i: (bb * n_t + i, 0)),
        out_shape=jax.ShapeDtypeStruct((batch * seq, D_MODEL), F32),
        scratch_shapes=[pltpu.VMEM((tm + 2 * FFN_HALO, D_MODEL), BF16),
                        pltpu.VMEM((2, tm + 2 * FFN_HALO, 2 * FFN_CHUNK), F32),
                        pltpu.VMEM((tm, D_FF), BF16)],
        compiler_params=_params(2),
        name="ffn_ln",
    )(x2d, x2d, x2d, wup, cw, cb, wdn, g[None, :], b[None, :])


def _trunk(x, mem, p):
    batch, seq, _ = x.shape
    n_mem = mem.shape[1]
    x2d = x.reshape(batch * seq, D_MODEL)
    mem2d = mem.reshape(batch * n_mem, D_MODEL)
    for l in range(DEPTH):
        i = l // 2
        if l % 2 == 0:
            qkva, qb, kvb = _proj_ab(x2d, seq, p["ab_w_in"][i], p["gqa_q_gain"][i], p["gqa_k_gain"][i])
            oa = _na_attn(qkva, batch, seq, p["na_bias"][i])
            ob = _gqa_attn(qb, kvb, batch, seq)
            x2d = _out_ln(x2d, oa, ob, p["ab_w_out"][i], p["ln1_g"][l], p["ln1_b"][l])
        else:
            x2d = _pool_ln(x2d, batch, seq, p["pool_w"][i], p["pool_b"][i], p["pool_scale"][i],
                           p["ln1_g"][l], p["ln1_b"][l])
        kt, v = _kv_proj(mem2d, batch, n_mem, p["xa_wkv"][l])
        x2d = _xattn_ln(x2d, batch, seq, kt, v, p["xa_wq"][l], p["xa_wo"][l],
                        p["ln2_g"][l], p["ln2_b"][l])
        wup, cw, cb = p["ffn"][l]
        x2d = _ffn_ln(x2d, batch, seq, wup, cw, cb, p["ffn_w_down"][l], p["ln3_g"][l], p["ln3_b"][l])
    return x2d.reshape(batch, seq, D_MODEL)


def kernel(x_prompt, x_sample, mem_prompt, mem_sample, ab_w_in, na_rpb, gqa_q_gain, gqa_k_gain,
           ab_w_out, pool_w, pool_b, pool_scale, ln1_g, ln1_b, xa_wq, xa_wkv, xa_wo, ln2_g, ln2_b,
           ffn_w_up, ffn_conv_w, ffn_conv_b, ffn_w_down, ln3_g, ln3_b):
    p = dict(
        ab_w_in=ab_w_in.astype(BF16), gqa_q_gain=gqa_q_gain, gqa_k_gain=gqa_k_gain,
        na_bias=[_na_bias_table(na_rpb[i]) for i in range(na_rpb.shape[0])],
        ab_w_out=ab_w_out.astype(BF16),
        pool_w=pool_w.astype(BF16), pool_b=pool_b, pool_scale=pool_scale,
        ln1_g=ln1_g, ln1_b=ln1_b, ln2_g=ln2_g, ln2_b=ln2_b, ln3_g=ln3_g, ln3_b=ln3_b,
        xa_wq=xa_wq.astype(BF16), xa_wkv=xa_wkv.astype(BF16), xa_wo=xa_wo.astype(BF16),
        ffn=[_ffn_weights(ffn_w_up[l], ffn_conv_w[l], ffn_conv_b[l]) for l in range(DEPTH)],
        ffn_w_down=ffn_w_down.astype(BF16),
    )
    return (_trunk(x_prompt, mem_prompt, p), _trunk(x_sample, mem_sample, p))
```

```python
import functools

import jax
import jax.numpy as jnp
from jax import lax
from jax.experimental import pallas as pl
from jax.experimental.pallas import tpu as pltpu

F32 = jnp.float32
BF16 = jnp.bfloat16

D_MODEL = 1024
DEPTH = 2
GRID_W = 64
HEAD_DIM = 64
NA_HEADS = 8
NA_WIN_H = 8
NA_WIN_W = 16
GQA_HEADS = 8
GQA_KV_HEADS = 2
ROPE_THETA = 10000.0
POOL_WINDOWS = (2, 4, 8, 16)
POOL_GROUPS = 4
POOL_CH = D_MODEL // POOL_GROUPS
XA_HEADS = 4
XA_HEAD_DIM = D_MODEL // XA_HEADS
D_FF = 2816
LN_EPS = 1e-5
QK_EPS = 1e-6
NEG_INF = -1e30
DN_ALPHA = (2 * DEPTH) ** 0.25
NA_WIDTH = NA_HEADS * HEAD_DIM
GQA_Q_WIDTH = GQA_HEADS * HEAD_DIM
GQA_KV_WIDTH = GQA_KV_HEADS * HEAD_DIM
AB_IN = 3 * NA_WIDTH + GQA_Q_WIDTH + 2 * GQA_KV_WIDTH
ATTN_SCALE = HEAD_DIM ** -0.5
XA_SCALE = XA_HEAD_DIM ** -0.5
LOG2E = 1.4426950408889634

LANES = 128
BF16_ROWS = 16
F32_ROWS = 8
TOKEN_TILE = 512
GQA_SCORE_ELEMS = 1 << 20
NA_ROWS_PER_STEP = 4
FFN_CHUNK = 256
FFN_NCHUNK = D_FF // FFN_CHUNK
NA_KEYS = NA_WIN_H * GRID_W
VMEM_LIMIT_BYTES = 56 * 1024 * 1024


def _params(n_axes):
    return pltpu.CompilerParams(
        dimension_semantics=("arbitrary",) * n_axes,
        vmem_limit_bytes=VMEM_LIMIT_BYTES)


def _resident(shape):
    nd = len(shape)
    return pl.BlockSpec(shape, lambda *_: (0,) * nd, pipeline_mode=pl.Buffered(1))


def _dot(a, b):
    return jnp.dot(a, b, preferred_element_type=F32)


def _dot_nt(a, b):
    return lax.dot_general(a, b, (((1,), (1,)), ((), ())), preferred_element_type=F32)


def _layer_norm(z, g, b):
    mu = jnp.mean(z, axis=-1, keepdims=True)
    zc = z - mu
    var = jnp.mean(zc * zc, axis=-1, keepdims=True)
    return zc * lax.rsqrt(var + LN_EPS) * g + b


def _rope_partner(x):
    lane = lax.broadcasted_iota(jnp.int32, x.shape, 1)
    first_half = (lane & 16) == 0
    return jnp.where(first_half, pltpu.roll(x, LANES - 16, 1), pltpu.roll(x, 16, 1))


def _norm_rope(x, seg_ones, gain, cos, sin):
    x2 = x * x
    hi = x2.astype(BF16)
    lo = (x2 - hi.astype(F32)).astype(BF16)
    ss = _dot(hi, seg_ones) + _dot(lo, seg_ones)
    xn = x * lax.rsqrt(ss * (1.0 / HEAD_DIM) + QK_EPS) * gain
    return xn * cos + _rope_partner(xn) * sin


def _proj_ab_kernel(x_ref, w_ref, cos_ref, sin_ref, ones_ref, qg_ref, kg_ref,
                    qkva_ref, qb_ref, kvb_ref):
    xb = x_ref[...].astype(BF16)
    ha = _dot(xb, w_ref[:, :3 * NA_WIDTH])
    qkva_ref[:, :NA_WIDTH] = (ha[:, :NA_WIDTH] * (ATTN_SCALE * LOG2E)).astype(BF16)
    qkva_ref[:, NA_WIDTH:] = ha[:, NA_WIDTH:].astype(BF16)
    hb = _dot(xb, w_ref[:, 3 * NA_WIDTH:])
    cos = cos_ref[...]
    sin = sin_ref[...]
    ones = ones_ref[...]
    for c in range(GQA_Q_WIDTH // LANES):
        sl = slice(c * LANES, (c + 1) * LANES)
        q = _norm_rope(hb[:, sl], ones, qg_ref[...], cos, sin)
        qb_ref[:, sl] = (q * (ATTN_SCALE * LOG2E)).astype(BF16)
    k0 = GQA_Q_WIDTH
    kvb_ref[:, :GQA_KV_WIDTH] = hb[:, k0 + GQA_KV_WIDTH:].astype(BF16)
    kvb_ref[:, GQA_KV_WIDTH:2 * GQA_KV_WIDTH] = jnp.ones((x_ref.shape[0], GQA_KV_WIDTH), BF16)
    kvb_ref[:, 2 * GQA_KV_WIDTH:] = _norm_rope(
        hb[:, k0:k0 + GQA_KV_WIDTH], ones, kg_ref[...], cos, sin).astype(BF16)


def _rope_tables(seq):
    t = jnp.arange(seq)
    row = (t // GRID_W).astype(F32)
    col = (t % GRID_W).astype(F32)
    quarter = HEAD_DIM // 4
    inv_freq = ROPE_THETA ** (-jnp.arange(quarter, dtype=F32) / quarter)
    ar = row[:, None] * inv_freq[None, :]
    ac = col[:, None] * inv_freq[None, :]
    cos = jnp.concatenate([jnp.cos(ar), jnp.cos(ar), jnp.cos(ac), jnp.cos(ac)], axis=-1)
    sin = jnp.concatenate([-jnp.sin(ar), jnp.sin(ar), -jnp.sin(ac), jnp.sin(ac)], axis=-1)
    reps = LANES // HEAD_DIM
    return jnp.tile(cos, (1, reps)), jnp.tile(sin, (1, reps))


def _proj_ab(x2d, seq, w_in, q_gain, k_gain):
    tokens = x2d.shape[0]
    tm = TOKEN_TILE
    tiles_per_seq = seq // tm
    cos, sin = _rope_tables(seq)
    seg = jnp.arange(LANES) // HEAD_DIM
    seg_ones = (seg[:, None] == seg[None, :]).astype(BF16)
    reps = LANES // HEAD_DIM
    qg = jnp.tile(q_gain, reps)[None, :]
    kg = jnp.tile(k_gain, reps)[None, :]
    tile = lambda width: pl.BlockSpec((tm, width), lambda i: (i, 0))
    table = pl.BlockSpec((tm, LANES), lambda i: (i % tiles_per_seq, 0))
    return pl.pallas_call(
        _proj_ab_kernel,
        grid=(tokens // tm,),
        in_specs=[tile(D_MODEL), _resident((D_MODEL, AB_IN)), table, table,
                  _resident((LANES, LANES)), _resident((1, LANES)), _resident((1, LANES))],
        out_specs=[tile(3 * NA_WIDTH), tile(GQA_Q_WIDTH), tile(3 * GQA_KV_WIDTH)],
        out_shape=[jax.ShapeDtypeStruct((tokens, 3 * NA_WIDTH), BF16),
                   jax.ShapeDtypeStruct((tokens, GQA_Q_WIDTH), BF16),
                   jax.ShapeDtypeStruct((tokens, 3 * GQA_KV_WIDTH), BF16)],
        compiler_params=_params(1),
        name="proj_ab",
    )(x2d, w_in, cos, sin, seg_ones, qg, kg)


def _na_bias_table(rpb):
    c = jnp.arange(GRID_W)
    cs = jnp.clip(c - NA_WIN_W // 2, 0, GRID_W - NA_WIN_W)
    col_valid = (c[None, :] >= cs[:, None]) & (c[None, :] < cs[:, None] + NA_WIN_W)
    dc = c[None, :] - c[:, None] + (NA_WIN_W - 1)
    toeplitz = jnp.zeros(rpb.shape[:2] + (GRID_W, GRID_W), F32)
    for j in range(2 * NA_WIN_W - 1):
        toeplitz = jnp.where(dc == j, rpb[:, :, j, None, None].astype(F32), toeplitz)
    toeplitz = jnp.where(col_valid, toeplitz * LOG2E, NEG_INF)
    bias = jnp.stack([toeplitz[:, NA_WIN_H - 1 - off:2 * NA_WIN_H - 1 - off]
                      for off in range(NA_WIN_H)])
    bias = jnp.transpose(bias, (0, 1, 3, 2, 4))
    return bias.reshape(NA_WIN_H, NA_HEADS // 2, 2 * GRID_W, NA_KEYS)


def _na_window_start(r, n_rows):
    return jnp.clip(r - NA_WIN_H // 2, 0, n_rows - NA_WIN_H)


def _na_kernel(q_ref, k_ref, v_ref, bias_ref, o_ref, *, n_rows):
    lower = lax.broadcasted_iota(jnp.int32, (GRID_W, LANES), 1) < HEAD_DIM
    for rr in range(NA_ROWS_PER_STEP):
        r = pl.program_id(1) * NA_ROWS_PER_STEP + rr
        first = _na_window_start(r, n_rows)
        start = pl.multiple_of(first * GRID_W, GRID_W)
        rows = slice(rr * GRID_W, (rr + 1) * GRID_W)
        for p in range(NA_HEADS // 2):
            sl = slice(p * LANES, (p + 1) * LANES)
            qp = q_ref[rows, sl]
            zero = jnp.zeros_like(qp)
            lhs = jnp.concatenate([jnp.where(lower, qp, zero), jnp.where(lower, zero, qp)], axis=0)
            k = k_ref[pl.ds(start, NA_KEYS), sl]
            v = v_ref[pl.ds(start, NA_KEYS), sl]
            s = _dot_nt(lhs, k) + bias_ref[r - first, p]
            e = jnp.exp2(s - jnp.max(s, axis=-1, keepdims=True))
            inv_l = 1.0 / jnp.sum(e, axis=-1, keepdims=True)
            o = _dot(e.astype(BF16), v) * inv_l
            o_ref[rows, sl] = jnp.where(lower, o[:GRID_W], o[GRID_W:]).astype(BF16)


def _na_attn(qkva, batch, seq, bias_table):
    n_rows = seq // GRID_W
    assert n_rows >= NA_WIN_H and n_rows % NA_ROWS_PER_STEP == 0
    n_steps = n_rows // NA_ROWS_PER_STEP
    tq = NA_ROWS_PER_STEP * GRID_W
    return pl.pallas_call(
        functools.partial(_na_kernel, n_rows=n_rows),
        grid=(batch, n_steps),
        in_specs=[
            pl.BlockSpec((tq, NA_WIDTH), lambda b, i: (b * n_steps + i, 0)),
            pl.BlockSpec((seq, NA_WIDTH), lambda b, i: (b, 1)),
            pl.BlockSpec((seq, NA_WIDTH), lambda b, i: (b, 2)),
            _resident(bias_table.shape),
        ],
        out_specs=pl.BlockSpec((tq, NA_WIDTH), lambda b, i: (b * n_steps + i, 0)),
        out_shape=jax.ShapeDtypeStruct((batch * seq, NA_WIDTH), BF16),
        compiler_params=_params(2),
        name="na_attn",
    )(qkva, qkva, qkva, bias_table)


def _gqa_kernel(q_ref, k_ref, v_ref, o_ref):
    tq = q_ref.shape[0]
    lower = lax.broadcasted_iota(jnp.int32, (tq, LANES), 1) < HEAD_DIM
    k = k_ref[...]
    v = v_ref[...]
    heads_per_kv = GQA_HEADS // GQA_KV_HEADS

    def head(h):
        kv_head = h // heads_per_kv
        swap = (h % 2) != kv_head
        qp = q_ref[:, (h // 2) * LANES:(h // 2 + 1) * LANES]
        if swap:
            qp = pltpu.roll(qp, HEAD_DIM, 1)
        keep = lower if kv_head == 0 else jnp.logical_not(lower)
        lhs = jnp.where(keep, qp, jnp.zeros_like(qp))
        s = _dot_nt(lhs, k)
        e = jnp.exp2(s - jnp.max(s, axis=-1, keepdims=True)).astype(BF16)
        ov = _dot(e, v)
        o = ov[:, :LANES] * (1.0 / ov[:, LANES:])
        return pltpu.roll(o, HEAD_DIM, 1) if swap else o

    for p in range(GQA_HEADS // 2):
        out = jnp.where(lower, head(2 * p), head(2 * p + 1))
        o_ref[:, p * LANES:(p + 1) * LANES] = out.astype(BF16)


def _gqa_attn(qb, kvb, batch, seq):
    tq = min(seq, GQA_SCORE_ELEMS // seq)
    n_q = seq // tq
    return pl.pallas_call(
        _gqa_kernel,
        grid=(batch, n_q),
        in_specs=[
            pl.BlockSpec((tq, GQA_Q_WIDTH), lambda b, i: (b * n_q + i, 0)),
            pl.BlockSpec((seq, GQA_KV_WIDTH), lambda b, i: (b, 2)),
            pl.BlockSpec((seq, 2 * GQA_KV_WIDTH), lambda b, i: (b, 0)),
        ],
        out_specs=pl.BlockSpec((tq, GQA_Q_WIDTH), lambda b, i: (b * n_q + i, 0)),
        out_shape=jax.ShapeDtypeStruct((batch * seq, GQA_Q_WIDTH), BF16),
        compiler_params=_params(2),
        name="gqa_attn",
    )(qb, kvb, kvb)


def _out_ln_kernel(x_ref, oa_ref, ob_ref, w_ref, g_ref, b_ref, y_ref):
    m = _dot(oa_ref[...], w_ref[:NA_WIDTH, :]) + _dot(ob_ref[...], w_ref[NA_WIDTH:, :])
    y_ref[...] = _layer_norm(DN_ALPHA * x_ref[...] + m, g_ref[...], b_ref[...])


def _out_ln(x2d, oa, ob, w_out, g, b):
    tokens = x2d.shape[0]
    tm = TOKEN_TILE
    tile = lambda width: pl.BlockSpec((tm, width), lambda i: (i, 0))
    return pl.pallas_call(
        _out_ln_kernel,
        grid=(tokens // tm,),
        in_specs=[tile(D_MODEL), tile(NA_WIDTH), tile(GQA_Q_WIDTH),
                  _resident((NA_WIDTH + GQA_Q_WIDTH, D_MODEL)),
                  _resident((1, D_MODEL)), _resident((1, D_MODEL))],
        out_specs=tile(D_MODEL),
        out_shape=jax.ShapeDtypeStruct((tokens, D_MODEL), F32),
        compiler_params=_params(1),
        name="out_ln",
    )(x2d, oa, ob, w_out, g[None, :], b[None, :])


POOL_HALO = F32_ROWS


def _pool_ln_kernel(xp_ref, x_ref, xn_ref, w_ref, pb_ref, ps_ref, g_ref, b_ref, y_ref, xe_ref,
                    *, seq):
    i = pl.program_id(1)
    tm = x_ref.shape[0]
    x = x_ref[...]
    xe_ref[:POOL_HALO, :] = jnp.where(i > 0, xp_ref[...], 0.0)
    xe_ref[POOL_HALO:POOL_HALO + tm, :] = x
    xe_ref[POOL_HALO + tm:, :] = jnp.where(i < pl.num_programs(1) - 1, xn_ref[...], 0.0)
    pos = i * tm + lax.broadcasted_iota(jnp.int32, (tm, 1), 0)
    parts = []
    for gi, win in enumerate(POOL_WINDOWS):
        cols = slice(gi * POOL_CH, (gi + 1) * POOL_CH)
        lo, hi = -(win // 2), win - 1 - win // 2
        total = xe_ref[POOL_HALO + lo:POOL_HALO + lo + tm, cols]
        for d in range(lo + 1, hi + 1):
            total = total + xe_ref[POOL_HALO + d:POOL_HALO + d + tm, cols]
        cnt = jnp.minimum(pos + hi, seq - 1) - jnp.maximum(pos + lo, 0) + 1
        xi = x[:, cols]
        pooled = (total / cnt.astype(F32) - xi).astype(BF16)
        y = (_dot(pooled, w_ref[gi]) + pb_ref[:, cols]) * ps_ref[:, cols]
        parts.append(DN_ALPHA * xi + y)
    y_ref[...] = _layer_norm(jnp.concatenate(parts, axis=-1), g_ref[...], b_ref[...])


def _pool_ln(x2d, batch, seq, w, pb, ps, g, b):
    tm = TOKEN_TILE
    n_t = seq // tm
    halo_per_tile = tm // POOL_HALO
    n_halo = batch * seq // POOL_HALO
    prev_idx = lambda bb, i: (jnp.maximum((bb * n_t + i) * halo_per_tile - 1, 0), 0)
    next_idx = lambda bb, i: (jnp.minimum((bb * n_t + i + 1) * halo_per_tile, n_halo - 1), 0)
    return pl.pallas_call(
        functools.partial(_pool_ln_kernel, seq=seq),
        grid=(batch, n_t),
        in_specs=[
            pl.BlockSpec((POOL_HALO, D_MODEL), prev_idx),
            pl.BlockSpec((tm, D_MODEL), lambda bb, i: (bb * n_t + i, 0)),
            pl.BlockSpec((POOL_HALO, D_MODEL), next_idx),
            _resident((POOL_GROUPS, POOL_CH, POOL_CH)),
            _resident((1, D_MODEL)), _resident((1, D_MODEL)),
            _resident((1, D_MODEL)), _resident((1, D_MODEL)),
        ],
        out_specs=pl.BlockSpec((tm, D_MODEL), lambda bb, i: (bb * n_t + i, 0)),
        out_shape=jax.ShapeDtypeStruct((batch * seq, D_MODEL), F32),
        scratch_shapes=[pltpu.VMEM((tm + 2 * POOL_HALO, D_MODEL), F32)],
        compiler_params=_params(2),
        name="pool_ln",
    )(x2d, x2d, x2d, w, pb.reshape(1, D_MODEL), ps[None, :], g[None, :], b[None, :])


def _kv_proj_kernel(mem_ref, w_ref, kt_ref, v_ref):
    kv = _dot(mem_ref[...].astype(BF16), w_ref[...])
    kt_ref[...] = kv[:, :D_MODEL].T.astype(BF16)
    v_ref[...] = kv[:, D_MODEL:].astype(BF16)


def _kv_proj(mem2d, batch, n_mem, wkv):
    return pl.pallas_call(
        _kv_proj_kernel,
        grid=(batch,),
        in_specs=[pl.BlockSpec((n_mem, D_MODEL), lambda b: (b, 0)),
                  _resident((D_MODEL, 2 * D_MODEL))],
        out_specs=[pl.BlockSpec((None, D_MODEL, n_mem), lambda b: (b, 0, 0)),
                   pl.BlockSpec((None, n_mem, D_MODEL), lambda b: (b, 0, 0))],
        out_shape=[jax.ShapeDtypeStruct((batch, D_MODEL, n_mem), BF16),
                   jax.ShapeDtypeStruct((batch, n_mem, D_MODEL), BF16)],
        compiler_params=_params(1),
        name="kv_proj",
    )(mem2d, wkv)


def _xattn_ln_kernel(x_ref, kt_ref, v_ref, wq_ref, wo_ref, g_ref, b_ref, y_ref):
    x = x_ref[...]
    q = (_dot(x.astype(BF16), wq_ref[...]) * (XA_SCALE * LOG2E)).astype(BF16)
    outs = []
    for h in range(XA_HEADS):
        sl = slice(h * XA_HEAD_DIM, (h + 1) * XA_HEAD_DIM)
        s = _dot(q[:, sl], kt_ref[sl, :])
        e = jnp.exp2(s - jnp.max(s, axis=-1, keepdims=True))
        inv_l = 1.0 / jnp.sum(e, axis=-1, keepdims=True)
        outs.append((_dot(e.astype(BF16), v_ref[:, sl]) * inv_l).astype(BF16))
    m = _dot(jnp.concatenate(outs, axis=-1), wo_ref[...])
    y_ref[...] = _layer_norm(DN_ALPHA * x + m, g_ref[...], b_ref[...])


def _xattn_ln(x2d, batch, seq, kt, v, wq, wo, g, b):
    tm = TOKEN_TILE
    n_t = seq // tm
    n_mem = v.shape[1]
    return pl.pallas_call(
        _xattn_ln_kernel,
        grid=(batch, n_t),
        in_specs=[
            pl.BlockSpec((tm, D_MODEL), lambda bb, i: (bb * n_t + i, 0)),
            pl.BlockSpec((None, D_MODEL, n_mem), lambda bb, i: (bb, 0, 0)),
            pl.BlockSpec((None, n_mem, D_MODEL), lambda bb, i: (bb, 0, 0)),
            _resident((D_MODEL, D_MODEL)), _resident((D_MODEL, D_MODEL)),
            _resident((1, D_MODEL)), _resident((1, D_MODEL)),
        ],
        out_specs=pl.BlockSpec((tm, D_MODEL), lambda bb, i: (bb * n_t + i, 0)),
        out_shape=jax.ShapeDtypeStruct((batch * seq, D_MODEL), F32),
        compiler_params=_params(2),
        name="xattn_ln",
    )(x2d, kt, v, wq, wo, g[None, :], b[None, :])


FFN_HALO = BF16_ROWS


def _gelu_tanh(x):
    c = 0.7978845608028654
    return 0.5 * x * (1.0 + jnp.tanh(c * (x + 0.044715 * (x * x * x))))


def _ffn_ln_kernel(xp_ref, x_ref, xn_ref, wup_ref, cw_ref, cb_ref, wdn_ref, g_ref, b_ref, y_ref,
                   xe_ref, h_ref, act_ref):
    i = pl.program_id(1)
    tm = x_ref.shape[0]
    x = x_ref[...]
    xe_ref[:FFN_HALO, :] = jnp.where(i > 0, xp_ref[...], 0.0).astype(BF16)
    xe_ref[FFN_HALO:FFN_HALO + tm, :] = x.astype(BF16)
    xe_ref[FFN_HALO + tm:, :] = jnp.where(i < pl.num_programs(1) - 1, xn_ref[...], 0.0).astype(BF16)
    for c in range(FFN_NCHUNK):
        hbuf = h_ref.at[c % 2]
        hbuf[...] = _dot(xe_ref[...], wup_ref[c])
        cw = cw_ref[c]
        hc = (hbuf[FFN_HALO - 1:FFN_HALO - 1 + tm, :] * cw[0:1]
              + hbuf[FFN_HALO:FFN_HALO + tm, :] * cw[1:2]
              + hbuf[FFN_HALO + 1:FFN_HALO + 1 + tm, :] * cw[2:3]
              + cb_ref[c])
        act = _gelu_tanh(hc[:, FFN_CHUNK:]) * hc[:, :FFN_CHUNK]
        act_ref[:, c * FFN_CHUNK:(c + 1) * FFN_CHUNK] = act.astype(BF16)
    m = _dot(act_ref[...], wdn_ref[...])
    y_ref[...] = _layer_norm(DN_ALPHA * x + m, g_ref[...], b_ref[...])


def _ffn_weights(w_up, conv_w, conv_b):
    def pair(a):
        lead = a.shape[:-1]
        a = a.reshape(lead + (2, FFN_NCHUNK, FFN_CHUNK))
        a = jnp.moveaxis(a, -3, -2)
        a = a.reshape(lead + (FFN_NCHUNK, 2 * FFN_CHUNK))
        return jnp.moveaxis(a, -2, 0)
    return pair(w_up).astype(BF16), pair(conv_w), pair(conv_b[None, :])


def _ffn_ln(x2d, batch, seq, wup, cw, cb, wdn, g, b):
    tm = TOKEN_TILE
    n_t = seq // tm
    halo_per_tile = tm // FFN_HALO
    n_halo = batch * seq // FFN_HALO
    prev_idx = lambda bb, i: (jnp.maximum((bb * n_t + i) * halo_per_tile - 1, 0), 0)
    next_idx = lambda bb, i: (jnp.minimum((bb * n_t + i + 1) * halo_per_tile, n_halo - 1), 0)
    return pl.pallas_call(
        _ffn_ln_kernel,
        grid=(batch, n_t),
        in_specs=[
            pl.BlockSpec((FFN_HALO, D_MODEL), prev_idx),
            pl.BlockSpec((tm, D_MODEL), lambda bb, i: (bb * n_t + i, 0)),
            pl.BlockSpec((FFN_HALO, D_MODEL), next_idx),
            _resident((FFN_NCHUNK, D_MODEL, 2 * FFN_CHUNK)),
            _resident((FFN_NCHUNK, 3, 2 * FFN_CHUNK)),
            _resident((FFN_NCHUNK, 1, 2 * FFN_CHUNK)),
            _resident((D_FF, D_MODEL)),
            _resident((1, D_MODEL)), _resident((1, D_MODEL)),
        ],
        out_specs=pl.BlockSpec((tm, D_MODEL), lambda bb, i: (bb * n_t + i, 0)),
        out_shape=jax.ShapeDtypeStruct((batch * seq, D_MODEL), F32),
        scratch_shapes=[pltpu.VMEM((tm + 2 * FFN_HALO, D_MODEL), BF16),
                        pltpu.VMEM((2, tm + 2 * FFN_HALO, 2 * FFN_CHUNK), F32),
                        pltpu.VMEM((tm, D_FF), BF16)],
        compiler_params=_params(2),
        name="ffn_ln",
    )(x2d, x2d, x2d, wup, cw, cb, wdn, g[None, :], b[None, :])


def _trunk(x, mem, p):
    batch, seq, _ = x.shape
    n_mem = mem.shape[1]
    x2d = x.reshape(batch * seq, D_MODEL)
    mem2d = mem.reshape(batch * n_mem, D_MODEL)
    for l in range(DEPTH):
        i = l // 2
        if l % 2 == 0:
            qkva, qb, kvb = _proj_ab(x2d, seq, p["ab_w_in"][i], p["gqa_q_gain"][i], p["gqa_k_gain"][i])
            oa = _na_attn(qkva, batch, seq, p["na_bias"][i])
            ob = _gqa_attn(qb, kvb, batch, seq)
            x2d = _out_ln(x2d, oa, ob, p["ab_w_out"][i], p["ln1_g"][l], p["ln1_b"][l])
        else:
            x2d = _pool_ln(x2d, batch, seq, p["pool_w"][i], p["pool_b"][i], p["pool_scale"][i],
                           p["ln1_g"][l], p["ln1_b"][l])
        kt, v = _kv_proj(mem2d, batch, n_mem, p["xa_wkv"][l])
        x2d = _xattn_ln(x2d, batch, seq, kt, v, p["xa_wq"][l], p["xa_wo"][l],
                        p["ln2_g"][l], p["ln2_b"][l])
        wup, cw, cb = p["ffn"][l]
        x2d = _ffn_ln(x2d, batch, seq, wup, cw, cb, p["ffn_w_down"][l], p["ln3_g"][l], p["ln3_b"][l])
    return x2d.reshape(batch, seq, D_MODEL)


def kernel(x_prompt, x_sample, mem_prompt, mem_sample, ab_w_in, na_rpb, gqa_q_gain, gqa_k_gain,
           ab_w_out, pool_w, pool_b, pool_scale, ln1_g, ln1_b, xa_wq, xa_wkv, xa_wo, ln2_g, ln2_b,
           ffn_w_up, ffn_conv_w, ffn_conv_b, ffn_w_down, ln3_g, ln3_b):
    p = dict(
        ab_w_in=ab_w_in.astype(BF16), gqa_q_gain=gqa_q_gain, gqa_k_gain=gqa_k_gain,
        na_bias=[_na_bias_table(na_rpb[i]) for i in range(na_rpb.shape[0])],
        ab_w_out=ab_w_out.astype(BF16),
        pool_w=pool_w.astype(BF16), pool_b=pool_b, pool_scale=pool_scale,
        ln1_g=ln1_g, ln1_b=ln1_b, ln2_g=ln2_g, ln2_b=ln2_b, ln3_g=ln3_g, ln3_b=ln3_b,
        xa_wq=xa_wq.astype(BF16), xa_wkv=xa_wkv.astype(BF16), xa_wo=xa_wo.astype(BF16),
        ffn=[_ffn_weights(ffn_w_up[l], ffn_conv_w[l], ffn_conv_b[l]) for l in range(DEPTH)],
        ffn_w_down=ffn_w_down.astype(BF16),
    )
    return (_trunk(x_prompt, mem_prompt, p), _trunk(x_sample, mem_sample, p))
```

```python
import functools

import jax
import jax.numpy as jnp
from jax import lax
from jax.experimental import pallas as pl
from jax.experimental.pallas import tpu as pltpu

F32 = jnp.float32
BF16 = jnp.bfloat16

D_MODEL = 1024
DEPTH = 2
GRID_W = 64
HEAD_DIM = 64
NA_HEADS = 8
NA_WIN_H = 8
NA_WIN_W = 16
GQA_HEADS = 8
GQA_KV_HEADS = 2
ROPE_THETA = 10000.0
POOL_WINDOWS = (2, 4, 8, 16)
POOL_GROUPS = 4
POOL_CH = D_MODEL // POOL_GROUPS
XA_HEADS = 4
XA_HEAD_DIM = D_MODEL // XA_HEADS
D_FF = 2816
LN_EPS = 1e-5
QK_EPS = 1e-6
NEG_INF = -1e30
DN_ALPHA = (2 * DEPTH) ** 0.25
NA_WIDTH = NA_HEADS * HEAD_DIM
GQA_Q_WIDTH = GQA_HEADS * HEAD_DIM
GQA_KV_WIDTH = GQA_KV_HEADS * HEAD_DIM
AB_IN = 3 * NA_WIDTH + GQA_Q_WIDTH + 2 * GQA_KV_WIDTH
ATTN_SCALE = HEAD_DIM ** -0.5
XA_SCALE = XA_HEAD_DIM ** -0.5
LOG2E = 1.4426950408889634

LANES = 128
BF16_ROWS = 16
F32_ROWS = 8
TOKEN_TILE = 512
FFN_TOKEN_TILE = 1024
GQA_SCORE_ELEMS = 1 << 20
GQA_ROW_CHUNKS = 2
NA_ROWS_PER_STEP = 8
FFN_CHUNK = 256
FFN_NCHUNK = D_FF // FFN_CHUNK
NA_KEYS = NA_WIN_H * GRID_W
VMEM_LIMIT_BYTES = 56 * 1024 * 1024


def _params(n_axes):
    return pltpu.CompilerParams(
        dimension_semantics=("arbitrary",) * n_axes,
        vmem_limit_bytes=VMEM_LIMIT_BYTES)


def _resident(shape):
    nd = len(shape)
    return pl.BlockSpec(shape, lambda *_: (0,) * nd, pipeline_mode=pl.Buffered(1))


def _dot(a, b):
    return jnp.dot(a, b, preferred_element_type=F32)


def _dot_nt(a, b):
    return lax.dot_general(a, b, (((1,), (1,)), ((), ())), preferred_element_type=F32)


def _layer_norm(z, g, b):
    mu = jnp.mean(z, axis=-1, keepdims=True)
    zc = z - mu
    var = jnp.mean(zc * zc, axis=-1, keepdims=True)
    return zc * lax.rsqrt(var + LN_EPS) * g + b


def _rope_partner(x):
    lane = lax.broadcasted_iota(jnp.int32, x.shape, 1)
    first_half = (lane & 16) == 0
    return jnp.where(first_half, pltpu.roll(x, LANES - 16, 1), pltpu.roll(x, 16, 1))


def _norm_rope(x, seg_ones, gain, cos, sin):
    x2 = x * x
    hi = x2.astype(BF16)
    lo = (x2 - hi.astype(F32)).astype(BF16)
    ss = _dot(hi, seg_ones) + _dot(lo, seg_ones)
    xn = x * lax.rsqrt(ss * (1.0 / HEAD_DIM) + QK_EPS) * gain
    return xn * cos + _rope_partner(xn) * sin


def _proj_ab_kernel(x_ref, w_ref, cos_ref, sin_ref, ones_ref, qg_ref, kg_ref,
                    qkva_ref, qb_ref, kvb_ref):
    xb = x_ref[...].astype(BF16)
    ha = _dot(xb, w_ref[:, :3 * NA_WIDTH])
    qkva_ref[:, :NA_WIDTH] = (ha[:, :NA_WIDTH] * (ATTN_SCALE * LOG2E)).astype(BF16)
    qkva_ref[:, NA_WIDTH:] = ha[:, NA_WIDTH:].astype(BF16)
    hb = _dot(xb, w_ref[:, 3 * NA_WIDTH:])
    cos = cos_ref[...]
    sin = sin_ref[...]
    ones = ones_ref[...]
    for c in range(GQA_Q_WIDTH // LANES):
        sl = slice(c * LANES, (c + 1) * LANES)
        q = _norm_rope(hb[:, sl], ones, qg_ref[...], cos, sin)
        qb_ref[:, sl] = (q * (ATTN_SCALE * LOG2E)).astype(BF16)
    k0 = GQA_Q_WIDTH
    kvb_ref[:, :GQA_KV_WIDTH] = hb[:, k0 + GQA_KV_WIDTH:].astype(BF16)
    kvb_ref[:, GQA_KV_WIDTH:2 * GQA_KV_WIDTH] = jnp.ones((x_ref.shape[0], GQA_KV_WIDTH), BF16)
    kvb_ref[:, 2 * GQA_KV_WIDTH:] = _norm_rope(
        hb[:, k0:k0 + GQA_KV_WIDTH], ones, kg_ref[...], cos, sin).astype(BF16)


def _rope_tables(seq):
    t = jnp.arange(seq)
    row = (t // GRID_W).astype(F32)
    col = (t % GRID_W).astype(F32)
    quarter = HEAD_DIM // 4
    inv_freq = ROPE_THETA ** (-jnp.arange(quarter, dtype=F32) / quarter)
    ar = row[:, None] * inv_freq[None, :]
    ac = col[:, None] * inv_freq[None, :]
    cos = jnp.concatenate([jnp.cos(ar), jnp.cos(ar), jnp.cos(ac), jnp.cos(ac)], axis=-1)
    sin = jnp.concatenate([-jnp.sin(ar), jnp.sin(ar), -jnp.sin(ac), jnp.sin(ac)], axis=-1)
    reps = LANES // HEAD_DIM
    return jnp.tile(cos, (1, reps)), jnp.tile(sin, (1, reps))


def _proj_ab(x2d, seq, w_in, q_gain, k_gain):
    tokens = x2d.shape[0]
    tm = TOKEN_TILE
    tiles_per_seq = seq // tm
    cos, sin = _rope_tables(seq)
    seg = jnp.arange(LANES) // HEAD_DIM
    seg_ones = (seg[:, None] == seg[None, :]).astype(BF16)
    reps = LANES // HEAD_DIM
    qg = jnp.tile(q_gain, reps)[None, :]
    kg = jnp.tile(k_gain, reps)[None, :]
    tile = lambda width: pl.BlockSpec((tm, width), lambda i: (i, 0))
    table = pl.BlockSpec((tm, LANES), lambda i: (i % tiles_per_seq, 0))
    return pl.pallas_call(
        _proj_ab_kernel,
        grid=(tokens // tm,),
        in_specs=[tile(D_MODEL), _resident((D_MODEL, AB_IN)), table, table,
                  _resident((LANES, LANES)), _resident((1, LANES)), _resident((1, LANES))],
        out_specs=[tile(3 * NA_WIDTH), tile(GQA_Q_WIDTH), tile(3 * GQA_KV_WIDTH)],
        out_shape=[jax.ShapeDtypeStruct((tokens, 3 * NA_WIDTH), BF16),
                   jax.ShapeDtypeStruct((tokens, GQA_Q_WIDTH), BF16),
                   jax.ShapeDtypeStruct((tokens, 3 * GQA_KV_WIDTH), BF16)],
        compiler_params=_params(1),
        name="proj_ab",
    )(x2d, w_in, cos, sin, seg_ones, qg, kg)


def _na_bias_table(rpb):
    c = jnp.arange(GRID_W)
    cs = jnp.clip(c - NA_WIN_W // 2, 0, GRID_W - NA_WIN_W)
    col_valid = (c[None, :] >= cs[:, None]) & (c[None, :] < cs[:, None] + NA_WIN_W)
    dc = c[None, :] - c[:, None] + (NA_WIN_W - 1)
    toeplitz = jnp.zeros(rpb.shape[:2] + (GRID_W, GRID_W), F32)
    for j in range(2 * NA_WIN_W - 1):
        toeplitz = jnp.where(dc == j, rpb[:, :, j, None, None].astype(F32), toeplitz)
    toeplitz = jnp.where(col_valid, toeplitz * LOG2E, NEG_INF)
    bias = jnp.stack([toeplitz[:, NA_WIN_H - 1 - off:2 * NA_WIN_H - 1 - off]
                      for off in range(NA_WIN_H)])
    bias = jnp.transpose(bias, (0, 1, 3, 2, 4))
    return bias.reshape(NA_WIN_H, NA_HEADS // 2, 2 * GRID_W, NA_KEYS)


def _na_window_start(r, n_rows):
    return jnp.clip(r - NA_WIN_H // 2, 0, n_rows - NA_WIN_H)


def _na_kernel(q_ref, k_ref, v_ref, bias_ref, o_ref, *, n_rows):
    lower = lax.broadcasted_iota(jnp.int32, (GRID_W, LANES), 1) < HEAD_DIM
    for rr in range(NA_ROWS_PER_STEP):
        r = pl.program_id(1) * NA_ROWS_PER_STEP + rr
        first = _na_window_start(r, n_rows)
        start = pl.multiple_of(first * GRID_W, GRID_W)
        rows = slice(rr * GRID_W, (rr + 1) * GRID_W)
        for p in range(NA_HEADS // 2):
            sl = slice(p * LANES, (p + 1) * LANES)
            qp = q_ref[rows, sl]
            zero = jnp.zeros_like(qp)
            lhs = jnp.concatenate([jnp.where(lower, qp, zero), jnp.where(lower, zero, qp)], axis=0)
            k = k_ref[pl.ds(start, NA_KEYS), sl]
            v = v_ref[pl.ds(start, NA_KEYS), sl]
            s = _dot_nt(lhs, k) + bias_ref[r - first, p]
            e = jnp.exp2(s - jnp.max(s, axis=-1, keepdims=True))
            inv_l = 1.0 / jnp.sum(e, axis=-1, keepdims=True)
            o = _dot(e.astype(BF16), v) * inv_l
            o_ref[rows, sl] = jnp.where(lower, o[:GRID_W], o[GRID_W:]).astype(BF16)


def _na_attn(qkva, batch, seq, bias_table):
    n_rows = seq // GRID_W
    assert n_rows >= NA_WIN_H and n_rows % NA_ROWS_PER_STEP == 0
    n_steps = n_rows // NA_ROWS_PER_STEP
    tq = NA_ROWS_PER_STEP * GRID_W
    return pl.pallas_call(
        functools.partial(_na_kernel, n_rows=n_rows),
        grid=(batch, n_steps),
        in_specs=[
            pl.BlockSpec((tq, NA_WIDTH), lambda b, i: (b * n_steps + i, 0)),
            pl.BlockSpec((seq, NA_WIDTH), lambda b, i: (b, 1)),
            pl.BlockSpec((seq, NA_WIDTH), lambda b, i: (b, 2)),
            _resident(bias_table.shape),
        ],
        out_specs=pl.BlockSpec((tq, NA_WIDTH), lambda b, i: (b * n_steps + i, 0)),
        out_shape=jax.ShapeDtypeStruct((batch * seq, NA_WIDTH), BF16),
        compiler_params=_params(2),
        name="na_attn",
    )(qkva, qkva, qkva, bias_table)


def _gqa_kernel(q_ref, k_ref, v_ref, o_ref, *, n_chunks):
    tq = q_ref.shape[0] // n_chunks
    lower = lax.broadcasted_iota(jnp.int32, (tq, LANES), 1) < HEAD_DIM
    k = k_ref[...]
    v = v_ref[...]
    heads_per_kv = GQA_HEADS // GQA_KV_HEADS

    def head(h, rows):
        kv_head = h // heads_per_kv
        swap = (h % 2) != kv_head
        qp = q_ref[rows, (h // 2) * LANES:(h // 2 + 1) * LANES]
        if swap:
            qp = pltpu.roll(qp, HEAD_DIM, 1)
        keep = lower if kv_head == 0 else jnp.logical_not(lower)
        lhs = jnp.where(keep, qp, jnp.zeros_like(qp))
        s = _dot_nt(lhs, k)
        e = jnp.exp2(s - jnp.max(s, axis=-1, keepdims=True)).astype(BF16)
        ov = _dot(e, v)
        o = ov[:, :LANES] * (1.0 / ov[:, LANES:])
        return pltpu.roll(o, HEAD_DIM, 1) if swap else o

    for c in range(n_chunks):
        rows = slice(c * tq, (c + 1) * tq)
        for p in range(GQA_HEADS // 2):
            out = jnp.where(lower, head(2 * p, rows), head(2 * p + 1, rows))
            o_ref[rows, p * LANES:(p + 1) * LANES] = out.astype(BF16)


def _gqa_attn(qb, kvb, batch, seq):
    chunk = min(seq, GQA_SCORE_ELEMS // seq)
    n_chunks = min(GQA_ROW_CHUNKS, seq // chunk)
    tq = chunk * n_chunks
    n_q = seq // tq
    return pl.pallas_call(
        functools.partial(_gqa_kernel, n_chunks=n_chunks),
        grid=(batch, n_q),
        in_specs=[
            pl.BlockSpec((tq, GQA_Q_WIDTH), lambda b, i: (b * n_q + i, 0)),
            pl.BlockSpec((seq, GQA_KV_WIDTH), lambda b, i: (b, 2)),
            pl.BlockSpec((seq, 2 * GQA_KV_WIDTH), lambda b, i: (b, 0)),
        ],
        out_specs=pl.BlockSpec((tq, GQA_Q_WIDTH), lambda b, i: (b * n_q + i, 0)),
        out_shape=jax.ShapeDtypeStruct((batch * seq, GQA_Q_WIDTH), BF16),
        compiler_params=_params(2),
        name="gqa_attn",
    )(qb, kvb, kvb)


def _out_ln_kernel(x_ref, oa_ref, ob_ref, w_ref, g_ref, b_ref, y_ref):
    m = _dot(oa_ref[...], w_ref[:NA_WIDTH, :]) + _dot(ob_ref[...], w_ref[NA_WIDTH:, :])
    y_ref[...] = _layer_norm(DN_ALPHA * x_ref[...] + m, g_ref[...], b_ref[...])


def _out_ln(x2d, oa, ob, w_out, g, b):
    tokens = x2d.shape[0]
    tm = TOKEN_TILE
    tile = lambda width: pl.BlockSpec((tm, width), lambda i: (i, 0))
    return pl.pallas_call(
        _out_ln_kernel,
        grid=(tokens // tm,),
        in_specs=[tile(D_MODEL), tile(NA_WIDTH), tile(GQA_Q_WIDTH),
                  _resident((NA_WIDTH + GQA_Q_WIDTH, D_MODEL)),
                  _resident((1, D_MODEL)), _resident((1, D_MODEL))],
        out_specs=tile(D_MODEL),
        out_shape=jax.ShapeDtypeStruct((tokens, D_MODEL), F32),
        compiler_params=_params(1),
        name="out_ln",
    )(x2d, oa, ob, w_out, g[None, :], b[None, :])


POOL_HALO = F32_ROWS


def _pool_ln_kernel(xp_ref, x_ref, xn_ref, w_ref, pb_ref, ps_ref, g_ref, b_ref, y_ref, xe_ref,
                    *, seq):
    i = pl.program_id(1)
    tm = x_ref.shape[0]
    x = x_ref[...]
    xe_ref[:POOL_HALO, :] = jnp.where(i > 0, xp_ref[...], 0.0)
    xe_ref[POOL_HALO:POOL_HALO + tm, :] = x
    xe_ref[POOL_HALO + tm:, :] = jnp.where(i < pl.num_programs(1) - 1, xn_ref[...], 0.0)
    pos = i * tm + lax.broadcasted_iota(jnp.int32, (tm, 1), 0)
    parts = []
    for gi, win in enumerate(POOL_WINDOWS):
        cols = slice(gi * POOL_CH, (gi + 1) * POOL_CH)
        lo, hi = -(win // 2), win - 1 - win // 2
        total = xe_ref[POOL_HALO + lo:POOL_HALO + lo + tm, cols]
        for d in range(lo + 1, hi + 1):
            total = total + xe_ref[POOL_HALO + d:POOL_HALO + d + tm, cols]
        cnt = jnp.minimum(pos + hi, seq - 1) - jnp.maximum(pos + lo, 0) + 1
        xi = x[:, cols]
        pooled = (total / cnt.astype(F32) - xi).astype(BF16)
        y = (_dot(pooled, w_ref[gi]) + pb_ref[:, cols]) * ps_ref[:, cols]
        parts.append(DN_ALPHA * xi + y)
    y_ref[...] = _layer_norm(jnp.concatenate(parts, axis=-1), g_ref[...], b_ref[...])


def _pool_ln(x2d, batch, seq, w, pb, ps, g, b):
    tm = TOKEN_TILE
    n_t = seq // tm
    halo_per_tile = tm // POOL_HALO
    n_halo = batch * seq // POOL_HALO
    prev_idx = lambda bb, i: (jnp.maximum((bb * n_t + i) * halo_per_tile - 1, 0), 0)
    next_idx = lambda bb, i: (jnp.minimum((bb * n_t + i + 1) * halo_per_tile, n_halo - 1), 0)
    return pl.pallas_call(
        functools.partial(_pool_ln_kernel, seq=seq),
        grid=(batch, n_t),
        in_specs=[
            pl.BlockSpec((POOL_HALO, D_MODEL), prev_idx),
            pl.BlockSpec((tm, D_MODEL), lambda bb, i: (bb * n_t + i, 0)),
            pl.BlockSpec((POOL_HALO, D_MODEL), next_idx),
            _resident((POOL_GROUPS, POOL_CH, POOL_CH)),
            _resident((1, D_MODEL)), _resident((1, D_MODEL)),
            _resident((1, D_MODEL)), _resident((1, D_MODEL)),
        ],
        out_specs=pl.BlockSpec((tm, D_MODEL), lambda bb, i: (bb * n_t + i, 0)),
        out_shape=jax.ShapeDtypeStruct((batch * seq, D_MODEL), F32),
        scratch_shapes=[pltpu.VMEM((tm + 2 * POOL_HALO, D_MODEL), F32)],
        compiler_params=_params(2),
        name="pool_ln",
    )(x2d, x2d, x2d, w, pb.reshape(1, D_MODEL), ps[None, :], g[None, :], b[None, :])


def _kv_proj_kernel(mem_ref, w_ref, kt_ref, v_ref):
    kv = _dot(mem_ref[...].astype(BF16), w_ref[...])
    kt_ref[...] = kv[:, :D_MODEL].T.astype(BF16)
    v_ref[...] = kv[:, D_MODEL:].astype(BF16)


def _kv_proj(mem2d, batch, n_mem, wkv):
    return pl.pallas_call(
        _kv_proj_kernel,
        grid=(batch,),
        in_specs=[pl.BlockSpec((n_mem, D_MODEL), lambda b: (b, 0)),
                  _resident((D_MODEL, 2 * D_MODEL))],
        out_specs=[pl.BlockSpec((None, D_MODEL, n_mem), lambda b: (b, 0, 0)),
                   pl.BlockSpec((None, n_mem, D_MODEL), lambda b: (b, 0, 0))],
        out_shape=[jax.ShapeDtypeStruct((batch, D_MODEL, n_mem), BF16),
                   jax.ShapeDtypeStruct((batch, n_mem, D_MODEL), BF16)],
        compiler_params=_params(1),
        name="kv_proj",
    )(mem2d, wkv)


def _xattn_ln_kernel(x_ref, kt_ref, v_ref, wq_ref, wo_ref, g_ref, b_ref, y_ref):
    x = x_ref[...]
    q = (_dot(x.astype(BF16), wq_ref[...]) * (XA_SCALE * LOG2E)).astype(BF16)
    outs = []
    for h in range(XA_HEADS):
        sl = slice(h * XA_HEAD_DIM, (h + 1) * XA_HEAD_DIM)
        s = _dot(q[:, sl], kt_ref[sl, :])
        e = jnp.exp2(s - jnp.max(s, axis=-1, keepdims=True))
        inv_l = 1.0 / jnp.sum(e, axis=-1, keepdims=True)
        outs.append((_dot(e.astype(BF16), v_ref[:, sl]) * inv_l).astype(BF16))
    m = _dot(jnp.concatenate(outs, axis=-1), wo_ref[...])
    y_ref[...] = _layer_norm(DN_ALPHA * x + m, g_ref[...], b_ref[...])


def _xattn_ln(x2d, batch, seq, kt, v, wq, wo, g, b):
    tm = TOKEN_TILE
    n_t = seq // tm
    n_mem = v.shape[1]
    return pl.pallas_call(
        _xattn_ln_kernel,
        grid=(batch, n_t),
        in_specs=[
            pl.BlockSpec((tm, D_MODEL), lambda bb, i: (bb * n_t + i, 0)),
            pl.BlockSpec((None, D_MODEL, n_mem), lambda bb, i: (bb, 0, 0)),
            pl.BlockSpec((None, n_mem, D_MODEL), lambda bb, i: (bb, 0, 0)),
            _resident((D_MODEL, D_MODEL)), _resident((D_MODEL, D_MODEL)),
            _resident((1, D_MODEL)), _resident((1, D_MODEL)),
        ],
        out_specs=pl.BlockSpec((tm, D_MODEL), lambda bb, i: (bb * n_t + i, 0)),
        out_shape=jax.ShapeDtypeStruct((batch * seq, D_MODEL), F32),
        compiler_params=_params(2),
        name="xattn_ln",
    )(x2d, kt, v, wq, wo, g[None, :], b[None, :])


FFN_HALO = BF16_ROWS


def _gelu_tanh(x):
    c = 0.7978845608028654
    return 0.5 * x * (1.0 + jnp.tanh(c * (x + 0.044715 * (x * x * x))))


def _ffn_ln_kernel(xp_ref, x_ref, xn_ref, wup_ref, cw_ref, cb_ref, wdn_ref, g_ref, b_ref, y_ref,
                   xe_ref, h_ref, act_ref):
    i = pl.program_id(1)
    tm = x_ref.shape[0]
    x = x_ref[...]
    xe_ref[:FFN_HALO, :] = jnp.where(i > 0, xp_ref[...], 0.0).astype(BF16)
    xe_ref[FFN_HALO:FFN_HALO + tm, :] = x.astype(BF16)
    xe_ref[FFN_HALO + tm:, :] = jnp.where(i < pl.num_programs(1) - 1, xn_ref[...], 0.0).astype(BF16)
    for c in range(FFN_NCHUNK):
        hbuf = h_ref.at[c % 2]
        hbuf[...] = _dot(xe_ref[...], wup_ref[c])
        cw = cw_ref[c]
        hc = (hbuf[FFN_HALO - 1:FFN_HALO - 1 + tm, :] * cw[0:1]
              + hbuf[FFN_HALO:FFN_HALO + tm, :] * cw[1:2]
              + hbuf[FFN_HALO + 1:FFN_HALO + 1 + tm, :] * cw[2:3]
              + cb_ref[c])
        act = _gelu_tanh(hc[:, FFN_CHUNK:]) * hc[:, :FFN_CHUNK]
        act_ref[:, c * FFN_CHUNK:(c + 1) * FFN_CHUNK] = act.astype(BF16)
    m = _dot(act_ref[...], wdn_ref[...])
    y_ref[...] = _layer_norm(DN_ALPHA * x + m, g_ref[...], b_ref[...])


def _ffn_weights(w_up, conv_w, conv_b):
    def pair(a):
        lead = a.shape[:-1]
        a = a.reshape(lead + (2, FFN_NCHUNK, FFN_CHUNK))
        a = jnp.moveaxis(a, -3, -2)
        a = a.reshape(lead + (FFN_NCHUNK, 2 * FFN_CHUNK))
        return jnp.moveaxis(a, -2, 0)
    return pair(w_up).astype(BF16), pair(conv_w), pair(conv_b[None, :])


def _ffn_ln(x2d, batch, seq, wup, cw, cb, wdn, g, b):
    tm = min(seq, FFN_TOKEN_TILE)
    n_t = seq // tm
    halo_per_tile = tm // FFN_HALO
    n_halo = batch * seq // FFN_HALO
    prev_idx = lambda bb, i: (jnp.maximum((bb * n_t + i) * halo_per_tile - 1, 0), 0)
    next_idx = lambda bb, i: (jnp.minimum((bb * n_t + i + 1) * halo_per_tile, n_halo - 1), 0)
    return pl.pallas_call(
        _ffn_ln_kernel,
        grid=(batch, n_t),
        in_specs=[
            pl.BlockSpec((FFN_HALO, D_MODEL), prev_idx),
            pl.BlockSpec((tm, D_MODEL), lambda bb, i: (bb * n_t + i, 0)),
            pl.BlockSpec((FFN_HALO, D_MODEL), next_idx),
            _resident((FFN_NCHUNK, D_MODEL, 2 * FFN_CHUNK)),
            _resident((FFN_NCHUNK, 3, 2 * FFN_CHUNK)),
            _resident((FFN_NCHUNK, 1, 2 * FFN_CHUNK)),
            _resident((D_FF, D_MODEL)),
            _resident((1, D_MODEL)), _resident((1, D_MODEL)),
        ],
        out_specs=pl.BlockSpec((tm, D_MODEL), lambda bb, i: (bb * n_t + i, 0)),
        out_shape=jax.ShapeDtypeStruct((batch * seq, D_MODEL), F32),
        scratch_shapes=[pltpu.VMEM((tm + 2 * FFN_HALO, D_MODEL), BF16),
                        pltpu.VMEM((2, tm + 2 * FFN_HALO, 2 * FFN_CHUNK), F32),
                        pltpu.VMEM((tm, D_FF), BF16)],
        compiler_params=_params(2),
        name="ffn_ln",
    )(x2d, x2d, x2d, wup, cw, cb, wdn, g[None, :], b[None, :])


def _trunk(x, mem, p):
    batch, seq, _ = x.shape
    n_mem = mem.shape[1]
    x2d = x.reshape(batch * seq, D_MODEL)
    mem2d = mem.reshape(batch * n_mem, D_MODEL)
    for l in range(DEPTH):
        i = l // 2
        if l % 2 == 0:
            qkva, qb, kvb = _proj_ab(x2d, seq, p["ab_w_in"][i], p["gqa_q_gain"][i], p["gqa_k_gain"][i])
            oa = _na_attn(qkva, batch, seq, p["na_bias"][i])
            ob = _gqa_attn(qb, kvb, batch, seq)
            x2d = _out_ln(x2d, oa, ob, p["ab_w_out"][i], p["ln1_g"][l], p["ln1_b"][l])
        else:
            x2d = _pool_ln(x2d, batch, seq, p["pool_w"][i], p["pool_b"][i], p["pool_scale"][i],
                           p["ln1_g"][l], p["ln1_b"][l])
        kt, v = _kv_proj(mem2d, batch, n_mem, p["xa_wkv"][l])
        x2d = _xattn_ln(x2d, batch, seq, kt, v, p["xa_wq"][l], p["xa_wo"][l],
                        p["ln2_g"][l], p["ln2_b"][l])
        wup, cw, cb = p["ffn"][l]
        x2d = _ffn_ln(x2d, batch, seq, wup, cw, cb, p["ffn_w_down"][l], p["ln3_g"][l], p["ln3_b"][l])
    return x2d.reshape(batch, seq, D_MODEL)


def kernel(x_prompt, x_sample, mem_prompt, mem_sample, ab_w_in, na_rpb, gqa_q_gain, gqa_k_gain,
           ab_w_out, pool_w, pool_b, pool_scale, ln1_g, ln1_b, xa_wq, xa_wkv, xa_wo, ln2_g, ln2_b,
           ffn_w_up, ffn_conv_w, ffn_conv_b, ffn_w_down, ln3_g, ln3_b):
    p = dict(
        ab_w_in=ab_w_in.astype(BF16), gqa_q_gain=gqa_q_gain, gqa_k_gain=gqa_k_gain,
        na_bias=[_na_bias_table(na_rpb[i]) for i in range(na_rpb.shape[0])],
        ab_w_out=ab_w_out.astype(BF16),
        pool_w=pool_w.astype(BF16), pool_b=pool_b, pool_scale=pool_scale,
        ln1_g=ln1_g, ln1_b=ln1_b, ln2_g=ln2_g, ln2_b=ln2_b, ln3_g=ln3_g, ln3_b=ln3_b,
        xa_wq=xa_wq.astype(BF16), xa_wkv=xa_wkv.astype(BF16), xa_wo=xa_wo.astype(BF16),
        ffn=[_ffn_weights(ffn_w_up[l], ffn_conv_w[l], ffn_conv_b[l]) for l in range(DEPTH)],
        ffn_w_down=ffn_w_down.astype(BF16),
    )
    return (_trunk(x_prompt, mem_prompt, p), _trunk(x_sample, mem_sample, p))
```

```python
import functools

import jax
import jax.numpy as jnp
from jax import lax
from jax.experimental import pallas as pl
from jax.experimental.pallas import tpu as pltpu

F32 = jnp.float32
BF16 = jnp.bfloat16

D_MODEL = 1024
DEPTH = 2
GRID_W = 64
HEAD_DIM = 64
NA_HEADS = 8
NA_WIN_H = 8
NA_WIN_W = 16
GQA_HEADS = 8
GQA_KV_HEADS = 2
ROPE_THETA = 10000.0
POOL_WINDOWS = (2, 4, 8, 16)
POOL_GROUPS = 4
POOL_CH = D_MODEL // POOL_GROUPS
XA_HEADS = 4
XA_HEAD_DIM = D_MODEL // XA_HEADS
D_FF = 2816
LN_EPS = 1e-5
QK_EPS = 1e-6
NEG_INF = -1e30
DN_ALPHA = (2 * DEPTH) ** 0.25
NA_WIDTH = NA_HEADS * HEAD_DIM
GQA_Q_WIDTH = GQA_HEADS * HEAD_DIM
GQA_KV_WIDTH = GQA_KV_HEADS * HEAD_DIM
AB_IN = 3 * NA_WIDTH + GQA_Q_WIDTH + 2 * GQA_KV_WIDTH
ATTN_SCALE = HEAD_DIM ** -0.5
XA_SCALE = XA_HEAD_DIM ** -0.5
LOG2E = 1.4426950408889634

LANES = 128
BF16_ROWS = 16
F32_ROWS = 8
TOKEN_TILE = 512
FFN_TOKEN_TILE = 1024
GQA_SCORE_ELEMS = 1 << 20
GQA_ROW_CHUNKS = 2
NA_ROWS_PER_STEP = 8
FFN_CHUNK = 256
FFN_NCHUNK = D_FF // FFN_CHUNK
NA_KEYS = NA_WIN_H * GRID_W
VMEM_LIMIT_BYTES = 56 * 1024 * 1024


def _params(n_axes):
    return pltpu.CompilerParams(
        dimension_semantics=("arbitrary",) * n_axes,
        vmem_limit_bytes=VMEM_LIMIT_BYTES)


def _resident(shape):
    nd = len(shape)
    return pl.BlockSpec(shape, lambda *_: (0,) * nd, pipeline_mode=pl.Buffered(1))


def _dot(a, b):
    return jnp.dot(a, b, preferred_element_type=F32)


def _dot_nt(a, b):
    return lax.dot_general(a, b, (((1,), (1,)), ((), ())), preferred_element_type=F32)


def _layer_norm(z, g, b):
    mu = jnp.mean(z, axis=-1, keepdims=True)
    zc = z - mu
    var = jnp.mean(zc * zc, axis=-1, keepdims=True)
    return zc * lax.rsqrt(var + LN_EPS) * g + b


def _rope_partner(x):
    lane = lax.broadcasted_iota(jnp.int32, x.shape, 1)
    first_half = (lane & 16) == 0
    return jnp.where(first_half, pltpu.roll(x, LANES - 16, 1), pltpu.roll(x, 16, 1))


def _norm_rope(x, seg_ones, gain, cos, sin):
    x2 = x * x
    hi = x2.astype(BF16)
    lo = (x2 - hi.astype(F32)).astype(BF16)
    ss = _dot(hi, seg_ones) + _dot(lo, seg_ones)
    xn = x * lax.rsqrt(ss * (1.0 / HEAD_DIM) + QK_EPS) * gain
    return xn * cos + _rope_partner(xn) * sin


def _proj_ab_kernel(x_ref, w_ref, cos_ref, sin_ref, ones_ref, qg_ref, kg_ref,
                    qkva_ref, qb_ref, kvb_ref):
    xb = x_ref[...].astype(BF16)
    ha = _dot(xb, w_ref[:, :3 * NA_WIDTH])
    qkva_ref[:, :NA_WIDTH] = (ha[:, :NA_WIDTH] * (ATTN_SCALE * LOG2E)).astype(BF16)
    qkva_ref[:, NA_WIDTH:] = ha[:, NA_WIDTH:].astype(BF16)
    hb = _dot(xb, w_ref[:, 3 * NA_WIDTH:])
    cos = cos_ref[...]
    sin = sin_ref[...]
    ones = ones_ref[...]
    for c in range(GQA_Q_WIDTH // LANES):
        sl = slice(c * LANES, (c + 1) * LANES)
        q = _norm_rope(hb[:, sl], ones, qg_ref[...], cos, sin)
        qb_ref[:, sl] = (q * (ATTN_SCALE * LOG2E)).astype(BF16)
    k0 = GQA_Q_WIDTH
    kvb_ref[:, :GQA_KV_WIDTH] = hb[:, k0 + GQA_KV_WIDTH:].astype(BF16)
    kvb_ref[:, GQA_KV_WIDTH:2 * GQA_KV_WIDTH] = jnp.ones((x_ref.shape[0], GQA_KV_WIDTH), BF16)
    kvb_ref[:, 2 * GQA_KV_WIDTH:] = _norm_rope(
        hb[:, k0:k0 + GQA_KV_WIDTH], ones, kg_ref[...], cos, sin).astype(BF16)


def _rope_tables(seq):
    t = jnp.arange(seq)
    row = (t // GRID_W).astype(F32)
    col = (t % GRID_W).astype(F32)
    quarter = HEAD_DIM // 4
    inv_freq = ROPE_THETA ** (-jnp.arange(quarter, dtype=F32) / quarter)
    ar = row[:, None] * inv_freq[None, :]
    ac = col[:, None] * inv_freq[None, :]
    cos = jnp.concatenate([jnp.cos(ar), jnp.cos(ar), jnp.cos(ac), jnp.cos(ac)], axis=-1)
    sin = jnp.concatenate([-jnp.sin(ar), jnp.sin(ar), -jnp.sin(ac), jnp.sin(ac)], axis=-1)
    reps = LANES // HEAD_DIM
    return jnp.tile(cos, (1, reps)), jnp.tile(sin, (1, reps))


def _proj_ab(x2d, seq, w_in, q_gain, k_gain):
    tokens = x2d.shape[0]
    tm = TOKEN_TILE
    tiles_per_seq = seq // tm
    cos, sin = _rope_tables(seq)
    seg = jnp.arange(LANES) // HEAD_DIM
    seg_ones = (seg[:, None] == seg[None, :]).astype(BF16)
    reps = LANES // HEAD_DIM
    qg = jnp.tile(q_gain, reps)[None, :]
    kg = jnp.tile(k_gain, reps)[None, :]
    tile = lambda width: pl.BlockSpec((tm, width), lambda i: (i, 0))
    table = pl.BlockSpec((tm, LANES), lambda i: (i % tiles_per_seq, 0))
    return pl.pallas_call(
        _proj_ab_kernel,
        grid=(tokens // tm,),
        in_specs=[tile(D_MODEL), _resident((D_MODEL, AB_IN)), table, table,
                  _resident((LANES, LANES)), _resident((1, LANES)), _resident((1, LANES))],
        out_specs=[tile(3 * NA_WIDTH), tile(GQA_Q_WIDTH), tile(3 * GQA_KV_WIDTH)],
        out_shape=[jax.ShapeDtypeStruct((tokens, 3 * NA_WIDTH), BF16),
                   jax.ShapeDtypeStruct((tokens, GQA_Q_WIDTH), BF16),
                   jax.ShapeDtypeStruct((tokens, 3 * GQA_KV_WIDTH), BF16)],
        compiler_params=_params(1),
        name="proj_ab",
    )(x2d, w_in, cos, sin, seg_ones, qg, kg)


def _na_bias_table(rpb):
    c = jnp.arange(GRID_W)
    cs = jnp.clip(c - NA_WIN_W // 2, 0, GRID_W - NA_WIN_W)
    col_valid = (c[None, :] >= cs[:, None]) & (c[None, :] < cs[:, None] + NA_WIN_W)
    dc = c[None, :] - c[:, None] + (NA_WIN_W - 1)
    toeplitz = jnp.zeros(rpb.shape[:2] + (GRID_W, GRID_W), F32)
    for j in range(2 * NA_WIN_W - 1):
        toeplitz = jnp.where(dc == j, rpb[:, :, j, None, None].astype(F32), toeplitz)
    toeplitz = jnp.where(col_valid, toeplitz * LOG2E, NEG_INF)
    bias = jnp.stack([toeplitz[:, NA_WIN_H - 1 - off:2 * NA_WIN_H - 1 - off]
                      for off in range(NA_WIN_H)])
    bias = jnp.transpose(bias, (0, 1, 3, 2, 4))
    return bias.reshape(NA_WIN_H, NA_HEADS // 2, 2 * GRID_W, NA_KEYS)


def _na_window_start(r, n_rows):
    return jnp.clip(r - NA_WIN_H // 2, 0, n_rows - NA_WIN_H)


def _na_kernel(q_ref, k_ref, v_ref, bias_ref, o_ref, *, n_rows):
    lower = lax.broadcasted_iota(jnp.int32, (GRID_W, LANES), 1) < HEAD_DIM
    for rr in range(NA_ROWS_PER_STEP):
        r = pl.program_id(1) * NA_ROWS_PER_STEP + rr
        first = _na_window_start(r, n_rows)
        start = pl.multiple_of(first * GRID_W, GRID_W)
        rows = slice(rr * GRID_W, (rr + 1) * GRID_W)
        for p in range(NA_HEADS // 2):
            sl = slice(p * LANES, (p + 1) * LANES)
            qp = q_ref[rows, sl]
            zero = jnp.zeros_like(qp)
            lhs = jnp.concatenate([jnp.where(lower, qp, zero), jnp.where(lower, zero, qp)], axis=0)
            k = k_ref[pl.ds(start, NA_KEYS), sl]
            v = v_ref[pl.ds(start, NA_KEYS), sl]
            s = _dot_nt(lhs, k) + bias_ref[r - first, p]
            e = jnp.exp2(s - jnp.max(s, axis=-1, keepdims=True))
            inv_l = 1.0 / jnp.sum(e, axis=-1, keepdims=True)
            o = _dot(e.astype(BF16), v) * inv_l
            o_ref[rows, sl] = jnp.where(lower, o[:GRID_W], o[GRID_W:]).astype(BF16)


def _na_attn(qkva, batch, seq, bias_table):
    n_rows = seq // GRID_W
    assert n_rows >= NA_WIN_H and n_rows % NA_ROWS_PER_STEP == 0
    n_steps = n_rows // NA_ROWS_PER_STEP
    tq = NA_ROWS_PER_STEP * GRID_W
    return pl.pallas_call(
        functools.partial(_na_kernel, n_rows=n_rows),
        grid=(batch, n_steps),
        in_specs=[
            pl.BlockSpec((tq, NA_WIDTH), lambda b, i: (b * n_steps + i, 0)),
            pl.BlockSpec((seq, NA_WIDTH), lambda b, i: (b, 1)),
            pl.BlockSpec((seq, NA_WIDTH), lambda b, i: (b, 2)),
            _resident(bias_table.shape),
        ],
        out_specs=pl.BlockSpec((tq, NA_WIDTH), lambda b, i: (b * n_steps + i, 0)),
        out_shape=jax.ShapeDtypeStruct((batch * seq, NA_WIDTH), BF16),
        compiler_params=_params(2),
        name="na_attn",
    )(qkva, qkva, qkva, bias_table)


def _gqa_kernel(q_ref, k_ref, v_ref, o_ref, *, n_chunks):
    tq = q_ref.shape[0] // n_chunks
    lower = lax.broadcasted_iota(jnp.int32, (tq, LANES), 1) < HEAD_DIM
    k = k_ref[...]
    v = v_ref[...]
    heads_per_kv = GQA_HEADS // GQA_KV_HEADS

    def head(h, rows):
        kv_head = h // heads_per_kv
        swap = (h % 2) != kv_head
        qp = q_ref[rows, (h // 2) * LANES:(h // 2 + 1) * LANES]
        if swap:
            qp = pltpu.roll(qp, HEAD_DIM, 1)
        keep = lower if kv_head == 0 else jnp.logical_not(lower)
        lhs = jnp.where(keep, qp, jnp.zeros_like(qp))
        s = _dot_nt(lhs, k)
        e = jnp.exp2(s - jnp.max(s, axis=-1, keepdims=True)).astype(BF16)
        ov = _dot(e, v)
        o = ov[:, :LANES] * (1.0 / ov[:, LANES:])
        return pltpu.roll(o, HEAD_DIM, 1) if swap else o

    for c in range(n_chunks):
        rows = slice(c * tq, (c + 1) * tq)
        for p in range(GQA_HEADS // 2):
            out = jnp.where(lower, head(2 * p, rows), head(2 * p + 1, rows))
            o_ref[rows, p * LANES:(p + 1) * LANES] = out.astype(BF16)


def _gqa_attn(qb, kvb, batch, seq):
    chunk = min(seq, GQA_SCORE_ELEMS // seq)
    n_chunks = min(GQA_ROW_CHUNKS, seq // chunk)
    tq = chunk * n_chunks
    n_q = seq // tq
    return pl.pallas_call(
        functools.partial(_gqa_kernel, n_chunks=n_chunks),
        grid=(batch, n_q),
        in_specs=[
            pl.BlockSpec((tq, GQA_Q_WIDTH), lambda b, i: (b * n_q + i, 0)),
            pl.BlockSpec((seq, GQA_KV_WIDTH), lambda b, i: (b, 2)),
            pl.BlockSpec((seq, 2 * GQA_KV_WIDTH), lambda b, i: (b, 0)),
        ],
        out_specs=pl.BlockSpec((tq, GQA_Q_WIDTH), lambda b, i: (b * n_q + i, 0)),
        out_shape=jax.ShapeDtypeStruct((batch * seq, GQA_Q_WIDTH), BF16),
        compiler_params=_params(2),
        name="gqa_attn",
    )(qb, kvb, kvb)


def _xattn_block(x, kt_ref, v_ref, wq_ref, wo_ref, g_ref, b_ref):
    q = (_dot(x.astype(BF16), wq_ref[...]) * (XA_SCALE * LOG2E)).astype(BF16)
    outs = []
    for h in range(XA_HEADS):
        sl = slice(h * XA_HEAD_DIM, (h + 1) * XA_HEAD_DIM)
        s = _dot(q[:, sl], kt_ref[sl, :])
        e = jnp.exp2(s - jnp.max(s, axis=-1, keepdims=True))
        inv_l = 1.0 / jnp.sum(e, axis=-1, keepdims=True)
        outs.append((_dot(e.astype(BF16), v_ref[:, sl]) * inv_l).astype(BF16))
    m = _dot(jnp.concatenate(outs, axis=-1), wo_ref[...])
    return _layer_norm(DN_ALPHA * x + m, g_ref[...], b_ref[...])


def _xattn_specs(n_mem):
    return [pl.BlockSpec((None, D_MODEL, n_mem), lambda bb, i: (bb, 0, 0)),
            pl.BlockSpec((None, n_mem, D_MODEL), lambda bb, i: (bb, 0, 0)),
            _resident((D_MODEL, D_MODEL)), _resident((D_MODEL, D_MODEL)),
            _resident((1, D_MODEL)), _resident((1, D_MODEL))]


def _out_xattn_kernel(x_ref, oa_ref, ob_ref, w_ref, g_ref, b_ref, *rest):
    *xattn_refs, y_ref = rest
    m = _dot(oa_ref[...], w_ref[:NA_WIDTH, :]) + _dot(ob_ref[...], w_ref[NA_WIDTH:, :])
    x1 = _layer_norm(DN_ALPHA * x_ref[...] + m, g_ref[...], b_ref[...])
    y_ref[...] = _xattn_block(x1, *xattn_refs)


def _out_xattn(x2d, oa, ob, batch, seq, w_out, g, b, xattn_operands):
    tm = TOKEN_TILE
    n_t = seq // tm
    n_mem = xattn_operands[1].shape[1]
    tile = lambda width: pl.BlockSpec((tm, width), lambda bb, i: (bb * n_t + i, 0))
    return pl.pallas_call(
        _out_xattn_kernel,
        grid=(batch, n_t),
        in_specs=[tile(D_MODEL), tile(NA_WIDTH), tile(GQA_Q_WIDTH),
                  _resident((NA_WIDTH + GQA_Q_WIDTH, D_MODEL)),
                  _resident((1, D_MODEL)), _resident((1, D_MODEL))] + _xattn_specs(n_mem),
        out_specs=tile(D_MODEL),
        out_shape=jax.ShapeDtypeStruct((batch * seq, D_MODEL), F32),
        compiler_params=_params(2),
        name="out_xattn",
    )(x2d, oa, ob, w_out, g[None, :], b[None, :], *xattn_operands)


POOL_HALO = F32_ROWS


def _pool_xattn_kernel(xp_ref, x_ref, xn_ref, w_ref, pb_ref, ps_ref, g_ref, b_ref, *rest, seq):
    *xattn_refs, y_ref, xe_ref = rest
    i = pl.program_id(1)
    tm = x_ref.shape[0]
    x = x_ref[...]
    xe_ref[:POOL_HALO, :] = jnp.where(i > 0, xp_ref[...], 0.0)
    xe_ref[POOL_HALO:POOL_HALO + tm, :] = x
    xe_ref[POOL_HALO + tm:, :] = jnp.where(i < pl.num_programs(1) - 1, xn_ref[...], 0.0)
    pos = i * tm + lax.broadcasted_iota(jnp.int32, (tm, 1), 0)
    parts = []
    for gi, win in enumerate(POOL_WINDOWS):
        cols = slice(gi * POOL_CH, (gi + 1) * POOL_CH)
        lo, hi = -(win // 2), win - 1 - win // 2
        total = xe_ref[POOL_HALO + lo:POOL_HALO + lo + tm, cols]
        for d in range(lo + 1, hi + 1):
            total = total + xe_ref[POOL_HALO + d:POOL_HALO + d + tm, cols]
        cnt = jnp.minimum(pos + hi, seq - 1) - jnp.maximum(pos + lo, 0) + 1
        xi = x[:, cols]
        pooled = (total / cnt.astype(F32) - xi).astype(BF16)
        y = (_dot(pooled, w_ref[gi]) + pb_ref[:, cols]) * ps_ref[:, cols]
        parts.append(DN_ALPHA * xi + y)
    x1 = _layer_norm(jnp.concatenate(parts, axis=-1), g_ref[...], b_ref[...])
    y_ref[...] = _xattn_block(x1, *xattn_refs)


def _pool_xattn(x2d, batch, seq, w, pb, ps, g, b, xattn_operands):
    tm = TOKEN_TILE
    n_t = seq // tm
    n_mem = xattn_operands[1].shape[1]
    halo_per_tile = tm // POOL_HALO
    n_halo = batch * seq // POOL_HALO
    prev_idx = lambda bb, i: (jnp.maximum((bb * n_t + i) * halo_per_tile - 1, 0), 0)
    next_idx = lambda bb, i: (jnp.minimum((bb * n_t + i + 1) * halo_per_tile, n_halo - 1), 0)
    return pl.pallas_call(
        functools.partial(_pool_xattn_kernel, seq=seq),
        grid=(batch, n_t),
        in_specs=[
            pl.BlockSpec((POOL_HALO, D_MODEL), prev_idx),
            pl.BlockSpec((tm, D_MODEL), lambda bb, i: (bb * n_t + i, 0)),
            pl.BlockSpec((POOL_HALO, D_MODEL), next_idx),
            _resident((POOL_GROUPS, POOL_CH, POOL_CH)),
            _resident((1, D_MODEL)), _resident((1, D_MODEL)),
            _resident((1, D_MODEL)), _resident((1, D_MODEL)),
        ] + _xattn_specs(n_mem),
        out_specs=pl.BlockSpec((tm, D_MODEL), lambda bb, i: (bb * n_t + i, 0)),
        out_shape=jax.ShapeDtypeStruct((batch * seq, D_MODEL), F32),
        scratch_shapes=[pltpu.VMEM((tm + 2 * POOL_HALO, D_MODEL), F32)],
        compiler_params=_params(2),
        name="pool_xattn",
    )(x2d, x2d, x2d, w, pb.reshape(1, D_MODEL), ps[None, :], g[None, :], b[None, :],
      *xattn_operands)


def _kv_proj_kernel(mem_ref, w_ref, kt_ref, v_ref):
    kv = _dot(mem_ref[...].astype(BF16), w_ref[...])
    kt_ref[...] = kv[:, :D_MODEL].T.astype(BF16)
    v_ref[...] = kv[:, D_MODEL:].astype(BF16)


def _kv_proj(mem2d, batch, n_mem, wkv):
    return pl.pallas_call(
        _kv_proj_kernel,
        grid=(batch,),
        in_specs=[pl.BlockSpec((n_mem, D_MODEL), lambda b: (b, 0)),
                  _resident((D_MODEL, 2 * D_MODEL))],
        out_specs=[pl.BlockSpec((None, D_MODEL, n_mem), lambda b: (b, 0, 0)),
                   pl.BlockSpec((None, n_mem, D_MODEL), lambda b: (b, 0, 0))],
        out_shape=[jax.ShapeDtypeStruct((batch, D_MODEL, n_mem), BF16),
                   jax.ShapeDtypeStruct((batch, n_mem, D_MODEL), BF16)],
        compiler_params=_params(1),
        name="kv_proj",
    )(mem2d, wkv)


FFN_HALO = BF16_ROWS


def _gelu_tanh(x):
    c = 0.7978845608028654
    return 0.5 * x * (1.0 + jnp.tanh(c * (x + 0.044715 * (x * x * x))))


def _ffn_ln_kernel(xp_ref, x_ref, xn_ref, wup_ref, cw_ref, cb_ref, wdn_ref, g_ref, b_ref, y_ref,
                   xe_ref, h_ref, act_ref):
    i = pl.program_id(1)
    tm = x_ref.shape[0]
    x = x_ref[...]
    xe_ref[:FFN_HALO, :] = jnp.where(i > 0, xp_ref[...], 0.0).astype(BF16)
    xe_ref[FFN_HALO:FFN_HALO + tm, :] = x.astype(BF16)
    xe_ref[FFN_HALO + tm:, :] = jnp.where(i < pl.num_programs(1) - 1, xn_ref[...], 0.0).astype(BF16)
    for c in range(FFN_NCHUNK):
        hbuf = h_ref.at[c % 2]
        hbuf[...] = _dot(xe_ref[...], wup_ref[c])
        cw = cw_ref[c]
        hc = (hbuf[FFN_HALO - 1:FFN_HALO - 1 + tm, :] * cw[0:1]
              + hbuf[FFN_HALO:FFN_HALO + tm, :] * cw[1:2]
              + hbuf[FFN_HALO + 1:FFN_HALO + 1 + tm, :] * cw[2:3]
              + cb_ref[c])
        act = _gelu_tanh(hc[:, FFN_CHUNK:]) * hc[:, :FFN_CHUNK]
        act_ref[:, c * FFN_CHUNK:(c + 1) * FFN_CHUNK] = act.astype(BF16)
    m = _dot(act_ref[...], wdn_ref[...])
    y_ref[...] = _layer_norm(DN_ALPHA * x + m, g_ref[...], b_ref[...])


def _ffn_weights(w_up, conv_w, conv_b):
    def pair(a):
        lead = a.shape[:-1]
        a = a.reshape(lead + (2, FFN_NCHUNK, FFN_CHUNK))
        a = jnp.moveaxis(a, -3, -2)
        a = a.reshape(lead + (FFN_NCHUNK, 2 * FFN_CHUNK))
        return jnp.moveaxis(a, -2, 0)
    return pair(w_up).astype(BF16), pair(conv_w), pair(conv_b[None, :])


def _ffn_ln(x2d, batch, seq, wup, cw, cb, wdn, g, b):
    tm = min(seq, FFN_TOKEN_TILE)
    n_t = seq // tm
    halo_per_tile = tm // FFN_HALO
    n_halo = batch * seq // FFN_HALO
    prev_idx = lambda bb, i: (jnp.maximum((bb * n_t + i) * halo_per_tile - 1, 0), 0)
    next_idx = lambda bb, i: (jnp.minimum((bb * n_t + i + 1) * halo_per_tile, n_halo - 1), 0)
    return pl.pallas_call(
        _ffn_ln_kernel,
        grid=(batch, n_t),
        in_specs=[
            pl.BlockSpec((FFN_HALO, D_MODEL), prev_idx),
            pl.BlockSpec((tm, D_MODEL), lambda bb, i: (bb * n_t + i, 0)),
            pl.BlockSpec((FFN_HALO, D_MODEL), next_idx),
            _resident((FFN_NCHUNK, D_MODEL, 2 * FFN_CHUNK)),
            _resident((FFN_NCHUNK, 3, 2 * FFN_CHUNK)),
            _resident((FFN_NCHUNK, 1, 2 * FFN_CHUNK)),
            _resident((D_FF, D_MODEL)),
            _resident((1, D_MODEL)), _resident((1, D_MODEL)),
        ],
        out_specs=pl.BlockSpec((tm, D_MODEL), lambda bb, i: (bb * n_t + i, 0)),
        out_shape=jax.ShapeDtypeStruct((batch * seq, D_MODEL), F32),
        scratch_shapes=[pltpu.VMEM((tm + 2 * FFN_HALO, D_MODEL), BF16),
                        pltpu.VMEM((2, tm + 2 * FFN_HALO, 2 * FFN_CHUNK), F32),
                        pltpu.VMEM((tm, D_FF), BF16)],
        compiler_params=_params(2),
        name="ffn_ln",
    )(x2d, x2d, x2d, wup, cw, cb, wdn, g[None, :], b[None, :])


def _trunk(x, mem, p):
    batch, seq, _ = x.shape
    n_mem = mem.shape[1]
    x2d = x.reshape(batch * seq, D_MODEL)
    mem2d = mem.reshape(batch * n_mem, D_MODEL)
    for l in range(DEPTH):
        i = l // 2
        kt, v = _kv_proj(mem2d, batch, n_mem, p["xa_wkv"][l])
        xattn_operands = (kt, v, p["xa_wq"][l], p["xa_wo"][l],
                          p["ln2_g"][l][None, :], p["ln2_b"][l][None, :])
        if l % 2 == 0:
            qkva, qb, kvb = _proj_ab(x2d, seq, p["ab_w_in"][i], p["gqa_q_gain"][i], p["gqa_k_gain"][i])
            oa = _na_attn(qkva, batch, seq, p["na_bias"][i])
            ob = _gqa_attn(qb, kvb, batch, seq)
            x2d = _out_xattn(x2d, oa, ob, batch, seq, p["ab_w_out"][i], p["ln1_g"][l], p["ln1_b"][l],
                             xattn_operands)
        else:
            x2d = _pool_xattn(x2d, batch, seq, p["pool_w"][i], p["pool_b"][i], p["pool_scale"][i],
                              p["ln1_g"][l], p["ln1_b"][l], xattn_operands)
        wup, cw, cb = p["ffn"][l]
        x2d = _ffn_ln(x2d, batch, seq, wup, cw, cb, p["ffn_w_down"][l], p["ln3_g"][l], p["ln3_b"][l])
    return x2d.reshape(batch, seq, D_MODEL)


def kernel(x_prompt, x_sample, mem_prompt, mem_sample, ab_w_in, na_rpb, gqa_q_gain, gqa_k_gain,
           ab_w_out, pool_w, pool_b, pool_scale, ln1_g, ln1_b, xa_wq, xa_wkv, xa_wo, ln2_g, ln2_b,
           ffn_w_up, ffn_conv_w, ffn_conv_b, ffn_w_down, ln3_g, ln3_b):
    p = dict(
        ab_w_in=ab_w_in.astype(BF16), gqa_q_gain=gqa_q_gain, gqa_k_gain=gqa_k_gain,
        na_bias=[_na_bias_table(na_rpb[i]) for i in range(na_rpb.shape[0])],
        ab_w_out=ab_w_out.astype(BF16),
        pool_w=pool_w.astype(BF16), pool_b=pool_b, pool_scale=pool_scale,
        ln1_g=ln1_g, ln1_b=ln1_b, ln2_g=ln2_g, ln2_b=ln2_b, ln3_g=ln3_g, ln3_b=ln3_b,
        xa_wq=xa_wq.astype(BF16), xa_wkv=xa_wkv.astype(BF16), xa_wo=xa_wo.astype(BF16),
        ffn=[_ffn_weights(ffn_w_up[l], ffn_conv_w[l], ffn_conv_b[l]) for l in range(DEPTH)],
        ffn_w_down=ffn_w_down.astype(BF16),
    )
    return (_trunk(x_prompt, mem_prompt, p), _trunk(x_sample, mem_sample, p))
```

```python
import functools

import jax
import jax.numpy as jnp
from jax import lax
from jax.experimental import pallas as pl
from jax.experimental.pallas import tpu as pltpu

F32 = jnp.float32
BF16 = jnp.bfloat16

D_MODEL = 1024
DEPTH = 2
GRID_W = 64
HEAD_DIM = 64
NA_HEADS = 8
NA_WIN_H = 8
NA_WIN_W = 16
GQA_HEADS = 8
GQA_KV_HEADS = 2
ROPE_THETA = 10000.0
POOL_WINDOWS = (2, 4, 8, 16)
POOL_GROUPS = 4
POOL_CH = D_MODEL // POOL_GROUPS
XA_HEADS = 4
XA_HEAD_DIM = D_MODEL // XA_HEADS
D_FF = 2816
LN_EPS = 1e-5
QK_EPS = 1e-6
NEG_INF = -1e30
DN_ALPHA = (2 * DEPTH) ** 0.25
NA_WIDTH = NA_HEADS * HEAD_DIM
GQA_Q_WIDTH = GQA_HEADS * HEAD_DIM
GQA_KV_WIDTH = GQA_KV_HEADS * HEAD_DIM
AB_IN = 3 * NA_WIDTH + GQA_Q_WIDTH + 2 * GQA_KV_WIDTH
ATTN_SCALE = HEAD_DIM ** -0.5
XA_SCALE = XA_HEAD_DIM ** -0.5
LOG2E = 1.4426950408889634

LANES = 128
BF16_ROWS = 16
F32_ROWS = 8
TOKEN_TILE = 512
FFN_TOKEN_TILE = 1024
GQA_SCORE_ELEMS = 1 << 20
GQA_ROW_CHUNKS = 4
NA_ROWS_PER_STEP = 8
FFN_CHUNK = 256
FFN_NCHUNK = D_FF // FFN_CHUNK
NA_KEYS = NA_WIN_H * GRID_W
VMEM_LIMIT_BYTES = 56 * 1024 * 1024


def _params(n_axes):
    return pltpu.CompilerParams(
        dimension_semantics=("arbitrary",) * n_axes,
        vmem_limit_bytes=VMEM_LIMIT_BYTES)


def _resident(shape):
    nd = len(shape)
    return pl.BlockSpec(shape, lambda *_: (0,) * nd, pipeline_mode=pl.Buffered(1))


def _dot(a, b):
    return jnp.dot(a, b, preferred_element_type=F32)


def _dot_nt(a, b):
    return lax.dot_general(a, b, (((1,), (1,)), ((), ())), preferred_element_type=F32)


def _layer_norm(z, g, b):
    mu = jnp.mean(z, axis=-1, keepdims=True)
    zc = z - mu
    var = jnp.mean(zc * zc, axis=-1, keepdims=True)
    return zc * lax.rsqrt(var + LN_EPS) * g + b


def _rope_partner(x):
    lane = lax.broadcasted_iota(jnp.int32, x.shape, 1)
    first_half = (lane & 16) == 0
    return jnp.where(first_half, pltpu.roll(x, LANES - 16, 1), pltpu.roll(x, 16, 1))


def _norm_rope(x, seg_ones, gain, cos, sin):
    x2 = x * x
    hi = x2.astype(BF16)
    lo = (x2 - hi.astype(F32)).astype(BF16)
    ss = _dot(hi, seg_ones) + _dot(lo, seg_ones)
    xn = x * lax.rsqrt(ss * (1.0 / HEAD_DIM) + QK_EPS) * gain
    return xn * cos + _rope_partner(xn) * sin


def _proj_ab_kernel(x_ref, w_ref, cos_ref, sin_ref, ones_ref, qg_ref, kg_ref,
                    qkva_ref, qb_ref, kvb_ref):
    xb = x_ref[...].astype(BF16)
    ha = _dot(xb, w_ref[:, :3 * NA_WIDTH])
    qkva_ref[:, :NA_WIDTH] = (ha[:, :NA_WIDTH] * (ATTN_SCALE * LOG2E)).astype(BF16)
    qkva_ref[:, NA_WIDTH:] = ha[:, NA_WIDTH:].astype(BF16)
    hb = _dot(xb, w_ref[:, 3 * NA_WIDTH:])
    cos = cos_ref[...]
    sin = sin_ref[...]
    ones = ones_ref[...]
    for c in range(GQA_Q_WIDTH // LANES):
        sl = slice(c * LANES, (c + 1) * LANES)
        q = _norm_rope(hb[:, sl], ones, qg_ref[...], cos, sin)
        qb_ref[:, sl] = (q * (ATTN_SCALE * LOG2E)).astype(BF16)
    k0 = GQA_Q_WIDTH
    kvb_ref[:, :GQA_KV_WIDTH] = hb[:, k0 + GQA_KV_WIDTH:].astype(BF16)
    kvb_ref[:, GQA_KV_WIDTH:2 * GQA_KV_WIDTH] = jnp.ones((x_ref.shape[0], GQA_KV_WIDTH), BF16)
    kvb_ref[:, 2 * GQA_KV_WIDTH:] = _norm_rope(
        hb[:, k0:k0 + GQA_KV_WIDTH], ones, kg_ref[...], cos, sin).astype(BF16)


def _rope_tables(seq):
    t = jnp.arange(seq)
    row = (t // GRID_W).astype(F32)
    col = (t % GRID_W).astype(F32)
    quarter = HEAD_DIM // 4
    inv_freq = ROPE_THETA ** (-jnp.arange(quarter, dtype=F32) / quarter)
    ar = row[:, None] * inv_freq[None, :]
    ac = col[:, None] * inv_freq[None, :]
    cos = jnp.concatenate([jnp.cos(ar), jnp.cos(ar), jnp.cos(ac), jnp.cos(ac)], axis=-1)
    sin = jnp.concatenate([-jnp.sin(ar), jnp.sin(ar), -jnp.sin(ac), jnp.sin(ac)], axis=-1)
    reps = LANES // HEAD_DIM
    return jnp.tile(cos, (1, reps)), jnp.tile(sin, (1, reps))


def _proj_ab(x2d, seq, w_in, q_gain, k_gain):
    tokens = x2d.shape[0]
    tm = TOKEN_TILE
    tiles_per_seq = seq // tm
    cos, sin = _rope_tables(seq)
    seg = jnp.arange(LANES) // HEAD_DIM
    seg_ones = (seg[:, None] == seg[None, :]).astype(BF16)
    reps = LANES // HEAD_DIM
    qg = jnp.tile(q_gain, reps)[None, :]
    kg = jnp.tile(k_gain, reps)[None, :]
    tile = lambda width: pl.BlockSpec((tm, width), lambda i: (i, 0))
    table = pl.BlockSpec((tm, LANES), lambda i: (i % tiles_per_seq, 0))
    return pl.pallas_call(
        _proj_ab_kernel,
        grid=(tokens // tm,),
        in_specs=[tile(D_MODEL), _resident((D_MODEL, AB_IN)), table, table,
                  _resident((LANES, LANES)), _resident((1, LANES)), _resident((1, LANES))],
        out_specs=[tile(3 * NA_WIDTH), tile(GQA_Q_WIDTH), tile(3 * GQA_KV_WIDTH)],
        out_shape=[jax.ShapeDtypeStruct((tokens, 3 * NA_WIDTH), BF16),
                   jax.ShapeDtypeStruct((tokens, GQA_Q_WIDTH), BF16),
                   jax.ShapeDtypeStruct((tokens, 3 * GQA_KV_WIDTH), BF16)],
        compiler_params=_params(1),
        name="proj_ab",
    )(x2d, w_in, cos, sin, seg_ones, qg, kg)


def _na_bias_table(rpb):
    c = jnp.arange(GRID_W)
    cs = jnp.clip(c - NA_WIN_W // 2, 0, GRID_W - NA_WIN_W)
    col_valid = (c[None, :] >= cs[:, None]) & (c[None, :] < cs[:, None] + NA_WIN_W)
    dc = c[None, :] - c[:, None] + (NA_WIN_W - 1)
    toeplitz = jnp.zeros(rpb.shape[:2] + (GRID_W, GRID_W), F32)
    for j in range(2 * NA_WIN_W - 1):
        toeplitz = jnp.where(dc == j, rpb[:, :, j, None, None].astype(F32), toeplitz)
    toeplitz = jnp.where(col_valid, toeplitz * LOG2E, NEG_INF)
    bias = jnp.stack([toeplitz[:, NA_WIN_H - 1 - off:2 * NA_WIN_H - 1 - off]
                      for off in range(NA_WIN_H)])
    bias = jnp.transpose(bias, (0, 1, 3, 2, 4))
    return bias.reshape(NA_WIN_H, NA_HEADS // 2, 2 * GRID_W, NA_KEYS)


def _na_window_start(r, n_rows):
    return jnp.clip(r - NA_WIN_H // 2, 0, n_rows - NA_WIN_H)


def _na_kernel(q_ref, k_ref, v_ref, bias_ref, o_ref, *, n_rows):
    lower = lax.broadcasted_iota(jnp.int32, (GRID_W, LANES), 1) < HEAD_DIM
    for rr in range(NA_ROWS_PER_STEP):
        r = pl.program_id(1) * NA_ROWS_PER_STEP + rr
        first = _na_window_start(r, n_rows)
        start = pl.multiple_of(first * GRID_W, GRID_W)
        rows = slice(rr * GRID_W, (rr + 1) * GRID_W)
        for p in range(NA_HEADS // 2):
            sl = slice(p * LANES, (p + 1) * LANES)
            qp = q_ref[rows, sl]
            zero = jnp.zeros_like(qp)
            lhs = jnp.concatenate([jnp.where(lower, qp, zero), jnp.where(lower, zero, qp)], axis=0)
            k = k_ref[pl.ds(start, NA_KEYS), sl]
            v = v_ref[pl.ds(start, NA_KEYS), sl]
            s = _dot_nt(lhs, k) + bias_ref[r - first, p]
            e = jnp.exp2(s - jnp.max(s, axis=-1, keepdims=True))
            inv_l = 1.0 / jnp.sum(e, axis=-1, keepdims=True)
            o = _dot(e.astype(BF16), v) * inv_l
            o_ref[rows, sl] = jnp.where(lower, o[:GRID_W], o[GRID_W:]).astype(BF16)


def _na_attn(qkva, batch, seq, bias_table):
    n_rows = seq // GRID_W
    assert n_rows >= NA_WIN_H and n_rows % NA_ROWS_PER_STEP == 0
    n_steps = n_rows // NA_ROWS_PER_STEP
    tq = NA_ROWS_PER_STEP * GRID_W
    return pl.pallas_call(
        functools.partial(_na_kernel, n_rows=n_rows),
        grid=(batch, n_steps),
        in_specs=[
            pl.BlockSpec((tq, NA_WIDTH), lambda b, i: (b * n_steps + i, 0)),
            pl.BlockSpec((seq, NA_WIDTH), lambda b, i: (b, 1)),
            pl.BlockSpec((seq, NA_WIDTH), lambda b, i: (b, 2)),
            _resident(bias_table.shape),
        ],
        out_specs=pl.BlockSpec((tq, NA_WIDTH), lambda b, i: (b * n_steps + i, 0)),
        out_shape=jax.ShapeDtypeStruct((batch * seq, NA_WIDTH), BF16),
        compiler_params=_params(2),
        name="na_attn",
    )(qkva, qkva, qkva, bias_table)


def _gqa_kernel(q_ref, k_ref, v_ref, o_ref, *, n_chunks):
    tq = q_ref.shape[0] // n_chunks
    lower = lax.broadcasted_iota(jnp.int32, (tq, LANES), 1) < HEAD_DIM
    k = k_ref[...]
    v = v_ref[...]
    heads_per_kv = GQA_HEADS // GQA_KV_HEADS

    def head(h, rows):
        kv_head = h // heads_per_kv
        swap = (h % 2) != kv_head
        qp = q_ref[rows, (h // 2) * LANES:(h // 2 + 1) * LANES]
        if swap:
            qp = pltpu.roll(qp, HEAD_DIM, 1)
        keep = lower if kv_head == 0 else jnp.logical_not(lower)
        lhs = jnp.where(keep, qp, jnp.zeros_like(qp))
        s = _dot_nt(lhs, k)
        e = jnp.exp2(s - jnp.max(s, axis=-1, keepdims=True)).astype(BF16)
        ov = _dot(e, v)
        o = ov[:, :LANES] * (1.0 / ov[:, LANES:])
        return pltpu.roll(o, HEAD_DIM, 1) if swap else o

    for c in range(n_chunks):
        rows = slice(c * tq, (c + 1) * tq)
        for p in range(GQA_HEADS // 2):
            out = jnp.where(lower, head(2 * p, rows), head(2 * p + 1, rows))
            o_ref[rows, p * LANES:(p + 1) * LANES] = out.astype(BF16)


def _gqa_attn(qb, kvb, batch, seq):
    chunk = min(seq, GQA_SCORE_ELEMS // seq)
    n_chunks = min(GQA_ROW_CHUNKS, seq // chunk)
    tq = chunk * n_chunks
    n_q = seq // tq
    return pl.pallas_call(
        functools.partial(_gqa_kernel, n_chunks=n_chunks),
        grid=(batch, n_q),
        in_specs=[
            pl.BlockSpec((tq, GQA_Q_WIDTH), lambda b, i: (b * n_q + i, 0)),
            pl.BlockSpec((seq, GQA_KV_WIDTH), lambda b, i: (b, 2)),
            pl.BlockSpec((seq, 2 * GQA_KV_WIDTH), lambda b, i: (b, 0)),
        ],
        out_specs=pl.BlockSpec((tq, GQA_Q_WIDTH), lambda b, i: (b * n_q + i, 0)),
        out_shape=jax.ShapeDtypeStruct((batch * seq, GQA_Q_WIDTH), BF16),
        compiler_params=_params(2),
        name="gqa_attn",
    )(qb, kvb, kvb)


def _xattn_block(x, kt_ref, v_ref, wq_ref, wo_ref, g_ref, b_ref):
    q = (_dot(x.astype(BF16), wq_ref[...]) * (XA_SCALE * LOG2E)).astype(BF16)
    outs = []
    for h in range(XA_HEADS):
        sl = slice(h * XA_HEAD_DIM, (h + 1) * XA_HEAD_DIM)
        s = _dot(q[:, sl], kt_ref[sl, :])
        e = jnp.exp2(s - jnp.max(s, axis=-1, keepdims=True))
        inv_l = 1.0 / jnp.sum(e, axis=-1, keepdims=True)
        outs.append((_dot(e.astype(BF16), v_ref[:, sl]) * inv_l).astype(BF16))
    m = _dot(jnp.concatenate(outs, axis=-1), wo_ref[...])
    return _layer_norm(DN_ALPHA * x + m, g_ref[...], b_ref[...])


def _xattn_specs(n_mem):
    return [pl.BlockSpec((None, D_MODEL, n_mem), lambda bb, i: (bb, 0, 0)),
            pl.BlockSpec((None, n_mem, D_MODEL), lambda bb, i: (bb, 0, 0)),
            _resident((D_MODEL, D_MODEL)), _resident((D_MODEL, D_MODEL)),
            _resident((1, D_MODEL)), _resident((1, D_MODEL))]


def _out_xattn_kernel(x_ref, oa_ref, ob_ref, w_ref, g_ref, b_ref, *rest):
    *xattn_refs, y_ref = rest
    m = _dot(oa_ref[...], w_ref[:NA_WIDTH, :]) + _dot(ob_ref[...], w_ref[NA_WIDTH:, :])
    x1 = _layer_norm(DN_ALPHA * x_ref[...] + m, g_ref[...], b_ref[...])
    y_ref[...] = _xattn_block(x1, *xattn_refs)


def _out_xattn(x2d, oa, ob, batch, seq, w_out, g, b, xattn_operands):
    tm = TOKEN_TILE
    n_t = seq // tm
    n_mem = xattn_operands[1].shape[1]
    tile = lambda width: pl.BlockSpec((tm, width), lambda bb, i: (bb * n_t + i, 0))
    return pl.pallas_call(
        _out_xattn_kernel,
        grid=(batch, n_t),
        in_specs=[tile(D_MODEL), tile(NA_WIDTH), tile(GQA_Q_WIDTH),
                  _resident((NA_WIDTH + GQA_Q_WIDTH, D_MODEL)),
                  _resident((1, D_MODEL)), _resident((1, D_MODEL))] + _xattn_specs(n_mem),
        out_specs=tile(D_MODEL),
        out_shape=jax.ShapeDtypeStruct((batch * seq, D_MODEL), F32),
        compiler_params=_params(2),
        name="out_xattn",
    )(x2d, oa, ob, w_out, g[None, :], b[None, :], *xattn_operands)


POOL_HALO = F32_ROWS


def _pool_xattn_kernel(xp_ref, x_ref, xn_ref, w_ref, pb_ref, ps_ref, g_ref, b_ref, *rest, seq):
    *xattn_refs, y_ref, xe_ref = rest
    i = pl.program_id(1)
    tm = x_ref.shape[0]
    x = x_ref[...]
    xe_ref[:POOL_HALO, :] = jnp.where(i > 0, xp_ref[...], 0.0)
    xe_ref[POOL_HALO:POOL_HALO + tm, :] = x
    xe_ref[POOL_HALO + tm:, :] = jnp.where(i < pl.num_programs(1) - 1, xn_ref[...], 0.0)
    pos = i * tm + lax.broadcasted_iota(jnp.int32, (tm, 1), 0)
    parts = []
    for gi, win in enumerate(POOL_WINDOWS):
        cols = slice(gi * POOL_CH, (gi + 1) * POOL_CH)
        lo, hi = -(win // 2), win - 1 - win // 2
        total = xe_ref[POOL_HALO + lo:POOL_HALO + lo + tm, cols]
        for d in range(lo + 1, hi + 1):
            total = total + xe_ref[POOL_HALO + d:POOL_HALO + d + tm, cols]
        cnt = jnp.minimum(pos + hi, seq - 1) - jnp.maximum(pos + lo, 0) + 1
        xi = x[:, cols]
        pooled = (total / cnt.astype(F32) - xi).astype(BF16)
        y = (_dot(pooled, w_ref[gi]) + pb_ref[:, cols]) * ps_ref[:, cols]
        parts.append(DN_ALPHA * xi + y)
    x1 = _layer_norm(jnp.concatenate(parts, axis=-1), g_ref[...], b_ref[...])
    y_ref[...] = _xattn_block(x1, *xattn_refs)


def _pool_xattn(x2d, batch, seq, w, pb, ps, g, b, xattn_operands):
    tm = TOKEN_TILE
    n_t = seq // tm
    n_mem = xattn_operands[1].shape[1]
    halo_per_tile = tm // POOL_HALO
    n_halo = batch * seq // POOL_HALO
    prev_idx = lambda bb, i: (jnp.maximum((bb * n_t + i) * halo_per_tile - 1, 0), 0)
    next_idx = lambda bb, i: (jnp.minimum((bb * n_t + i + 1) * halo_per_tile, n_halo - 1), 0)
    return pl.pallas_call(
        functools.partial(_pool_xattn_kernel, seq=seq),
        grid=(batch, n_t),
        in_specs=[
            pl.BlockSpec((POOL_HALO, D_MODEL), prev_idx),
            pl.BlockSpec((tm, D_MODEL), lambda bb, i: (bb * n_t + i, 0)),
            pl.BlockSpec((POOL_HALO, D_MODEL), next_idx),
            _resident((POOL_GROUPS, POOL_CH, POOL_CH)),
            _resident((1, D_MODEL)), _resident((1, D_MODEL)),
            _resident((1, D_MODEL)), _resident((1, D_MODEL)),
        ] + _xattn_specs(n_mem),
        out_specs=pl.BlockSpec((tm, D_MODEL), lambda bb, i: (bb * n_t + i, 0)),
        out_shape=jax.ShapeDtypeStruct((batch * seq, D_MODEL), F32),
        scratch_shapes=[pltpu.VMEM((tm + 2 * POOL_HALO, D_MODEL), F32)],
        compiler_params=_params(2),
        name="pool_xattn",
    )(x2d, x2d, x2d, w, pb.reshape(1, D_MODEL), ps[None, :], g[None, :], b[None, :],
      *xattn_operands)


def _kv_proj_kernel(mem_ref, w_ref, kt_ref, v_ref):
    kv = _dot(mem_ref[...].astype(BF16), w_ref[...])
    kt_ref[...] = kv[:, :D_MODEL].T.astype(BF16)
    v_ref[...] = kv[:, D_MODEL:].astype(BF16)


def _kv_proj(mem2d, batch, n_mem, wkv):
    return pl.pallas_call(
        _kv_proj_kernel,
        grid=(batch,),
        in_specs=[pl.BlockSpec((n_mem, D_MODEL), lambda b: (b, 0)),
                  _resident((D_MODEL, 2 * D_MODEL))],
        out_specs=[pl.BlockSpec((None, D_MODEL, n_mem), lambda b: (b, 0, 0)),
                   pl.BlockSpec((None, n_mem, D_MODEL), lambda b: (b, 0, 0))],
        out_shape=[jax.ShapeDtypeStruct((batch, D_MODEL, n_mem), BF16),
                   jax.ShapeDtypeStruct((batch, n_mem, D_MODEL), BF16)],
        compiler_params=_params(1),
        name="kv_proj",
    )(mem2d, wkv)


FFN_HALO = BF16_ROWS


def _gelu_tanh(x):
    c = 0.7978845608028654
    return 0.5 * x * (1.0 + jnp.tanh(c * (x + 0.044715 * (x * x * x))))


def _ffn_ln_kernel(xp_ref, x_ref, xn_ref, wup_ref, cw_ref, cb_ref, wdn_ref, g_ref, b_ref, y_ref,
                   xe_ref, h_ref, act_ref):
    i = pl.program_id(1)
    tm = x_ref.shape[0]
    x = x_ref[...]
    xe_ref[:FFN_HALO, :] = jnp.where(i > 0, xp_ref[...], 0.0).astype(BF16)
    xe_ref[FFN_HALO:FFN_HALO + tm, :] = x.astype(BF16)
    xe_ref[FFN_HALO + tm:, :] = jnp.where(i < pl.num_programs(1) - 1, xn_ref[...], 0.0).astype(BF16)
    for c in range(FFN_NCHUNK):
        hbuf = h_ref.at[c % 2]
        hbuf[...] = _dot(xe_ref[...], wup_ref[c])
        cw = cw_ref[c]
        hc = (hbuf[FFN_HALO - 1:FFN_HALO - 1 + tm, :] * cw[0:1]
              + hbuf[FFN_HALO:FFN_HALO + tm, :] * cw[1:2]
              + hbuf[FFN_HALO + 1:FFN_HALO + 1 + tm, :] * cw[2:3]
              + cb_ref[c])
        act = _gelu_tanh(hc[:, FFN_CHUNK:]) * hc[:, :FFN_CHUNK]
        act_ref[:, c * FFN_CHUNK:(c + 1) * FFN_CHUNK] = act.astype(BF16)
    m = _dot(act_ref[...], wdn_ref[...])
    y_ref[...] = _layer_norm(DN_ALPHA * x + m, g_ref[...], b_ref[...])


def _ffn_weights(w_up, conv_w, conv_b):
    def pair(a):
        lead = a.shape[:-1]
        a = a.reshape(lead + (2, FFN_NCHUNK, FFN_CHUNK))
        a = jnp.moveaxis(a, -3, -2)
        a = a.reshape(lead + (FFN_NCHUNK, 2 * FFN_CHUNK))
        return jnp.moveaxis(a, -2, 0)
    return pair(w_up.astype(BF16)), pair(conv_w), pair(conv_b[None, :])


def _ffn_ln(x2d, batch, seq, wup, cw, cb, wdn, g, b):
    tm = min(seq, FFN_TOKEN_TILE)
    n_t = seq // tm
    halo_per_tile = tm // FFN_HALO
    n_halo = batch * seq // FFN_HALO
    prev_idx = lambda bb, i: (jnp.maximum((bb * n_t + i) * halo_per_tile - 1, 0), 0)
    next_idx = lambda bb, i: (jnp.minimum((bb * n_t + i + 1) * halo_per_tile, n_halo - 1), 0)
    return pl.pallas_call(
        _ffn_ln_kernel,
        grid=(batch, n_t),
        in_specs=[
            pl.BlockSpec((FFN_HALO, D_MODEL), prev_idx),
            pl.BlockSpec((tm, D_MODEL), lambda bb, i: (bb * n_t + i, 0)),
            pl.BlockSpec((FFN_HALO, D_MODEL), next_idx),
            _resident((FFN_NCHUNK, D_MODEL, 2 * FFN_CHUNK)),
            _resident((FFN_NCHUNK, 3, 2 * FFN_CHUNK)),
            _resident((FFN_NCHUNK, 1, 2 * FFN_CHUNK)),
            _resident((D_FF, D_MODEL)),
            _resident((1, D_MODEL)), _resident((1, D_MODEL)),
        ],
        out_specs=pl.BlockSpec((tm, D_MODEL), lambda bb, i: (bb * n_t + i, 0)),
        out_shape=jax.ShapeDtypeStruct((batch * seq, D_MODEL), F32),
        scratch_shapes=[pltpu.VMEM((tm + 2 * FFN_HALO, D_MODEL), BF16),
                        pltpu.VMEM((2, tm + 2 * FFN_HALO, 2 * FFN_CHUNK), F32),
                        pltpu.VMEM((tm, D_FF), BF16)],
        compiler_params=_params(2),
        name="ffn_ln",
    )(x2d, x2d, x2d, wup, cw, cb, wdn, g[None, :], b[None, :])


def _trunk(x, mem, p):
    batch, seq, _ = x.shape
    n_mem = mem.shape[1]
    x2d = x.reshape(batch * seq, D_MODEL)
    mem2d = mem.reshape(batch * n_mem, D_MODEL)
    for l in range(DEPTH):
        i = l // 2
        kt, v = _kv_proj(mem2d, batch, n_mem, p["xa_wkv"][l])
        xattn_operands = (kt, v, p["xa_wq"][l], p["xa_wo"][l],
                          p["ln2_g"][l][None, :], p["ln2_b"][l][None, :])
        if l % 2 == 0:
            qkva, qb, kvb = _proj_ab(x2d, seq, p["ab_w_in"][i], p["gqa_q_gain"][i], p["gqa_k_gain"][i])
            oa = _na_attn(qkva, batch, seq, p["na_bias"][i])
            ob = _gqa_attn(qb, kvb, batch, seq)
            x2d = _out_xattn(x2d, oa, ob, batch, seq, p["ab_w_out"][i], p["ln1_g"][l], p["ln1_b"][l],
                             xattn_operands)
        else:
            x2d = _pool_xattn(x2d, batch, seq, p["pool_w"][i], p["pool_b"][i], p["pool_scale"][i],
                              p["ln1_g"][l], p["ln1_b"][l], xattn_operands)
        wup, cw, cb = p["ffn"][l]
        x2d = _ffn_ln(x2d, batch, seq, wup, cw, cb, p["ffn_w_down"][l], p["ln3_g"][l], p["ln3_b"][l])
    return x2d.reshape(batch, seq, D_MODEL)


def kernel(x_prompt, x_sample, mem_prompt, mem_sample, ab_w_in, na_rpb, gqa_q_gain, gqa_k_gain,
           ab_w_out, pool_w, pool_b, pool_scale, ln1_g, ln1_b, xa_wq, xa_wkv, xa_wo, ln2_g, ln2_b,
           ffn_w_up, ffn_conv_w, ffn_conv_b, ffn_w_down, ln3_g, ln3_b):
    p = dict(
        ab_w_in=ab_w_in.astype(BF16), gqa_q_gain=gqa_q_gain, gqa_k_gain=gqa_k_gain,
        na_bias=[_na_bias_table(na_rpb[i]) for i in range(na_rpb.shape[0])],
        ab_w_out=ab_w_out.astype(BF16),
        pool_w=pool_w.astype(BF16), pool_b=pool_b, pool_scale=pool_scale,
        ln1_g=ln1_g, ln1_b=ln1_b, ln2_g=ln2_g, ln2_b=ln2_b, ln3_g=ln3_g, ln3_b=ln3_b,
        xa_wq=xa_wq.astype(BF16), xa_wkv=xa_wkv.astype(BF16), xa_wo=xa_wo.astype(BF16),
        ffn=[_ffn_weights(ffn_w_up[l], ffn_conv_w[l], ffn_conv_b[l]) for l in range(DEPTH)],
        ffn_w_down=ffn_w_down.astype(BF16),
    )
    return (_trunk(x_prompt, mem_prompt, p), _trunk(x_sample, mem_sample, p))
```

```python
import functools

import jax
import jax.numpy as jnp
from jax import lax
from jax.experimental import pallas as pl
from jax.experimental.pallas import tpu as pltpu

F32 = jnp.float32
BF16 = jnp.bfloat16

D_MODEL = 1024
DEPTH = 2
GRID_W = 64
HEAD_DIM = 64
NA_HEADS = 8
NA_WIN_H = 8
NA_WIN_W = 16
GQA_HEADS = 8
GQA_KV_HEADS = 2
ROPE_THETA = 10000.0
POOL_WINDOWS = (2, 4, 8, 16)
POOL_GROUPS = 4
POOL_CH = D_MODEL // POOL_GROUPS
XA_HEADS = 4
XA_HEAD_DIM = D_MODEL // XA_HEADS
D_FF = 2816
LN_EPS = 1e-5
QK_EPS = 1e-6
NEG_INF = -1e30
DN_ALPHA = (2 * DEPTH) ** 0.25
NA_WIDTH = NA_HEADS * HEAD_DIM
GQA_Q_WIDTH = GQA_HEADS * HEAD_DIM
GQA_KV_WIDTH = GQA_KV_HEADS * HEAD_DIM
AB_IN = 3 * NA_WIDTH + GQA_Q_WIDTH + 2 * GQA_KV_WIDTH
ATTN_SCALE = HEAD_DIM ** -0.5
XA_SCALE = XA_HEAD_DIM ** -0.5
LOG2E = 1.4426950408889634

LANES = 128
BF16_ROWS = 16
F32_ROWS = 8
TOKEN_TILE = 512
FFN_TOKEN_TILE = 1024
GQA_SCORE_ELEMS = 1 << 20
GQA_ROW_CHUNKS = 2
NA_ROWS_PER_STEP = 8
FFN_CHUNK = 256
FFN_NCHUNK = D_FF // FFN_CHUNK
NA_KEYS = NA_WIN_H * GRID_W
VMEM_LIMIT_BYTES = 56 * 1024 * 1024


def _params(n_axes):
    return pltpu.CompilerParams(
        dimension_semantics=("arbitrary",) * n_axes,
        vmem_limit_bytes=VMEM_LIMIT_BYTES)


def _resident(shape):
    nd = len(shape)
    return pl.BlockSpec(shape, lambda *_: (0,) * nd, pipeline_mode=pl.Buffered(1))


def _dot(a, b):
    return jnp.dot(a, b, preferred_element_type=F32)


def _dot_nt(a, b):
    return lax.dot_general(a, b, (((1,), (1,)), ((), ())), preferred_element_type=F32)


def _layer_norm(z, g, b):
    mu = jnp.mean(z, axis=-1, keepdims=True)
    zc = z - mu
    var = jnp.mean(zc * zc, axis=-1, keepdims=True)
    return zc * lax.rsqrt(var + LN_EPS) * g + b


def _rope_partner(x):
    lane = lax.broadcasted_iota(jnp.int32, x.shape, 1)
    first_half = (lane & 16) == 0
    return jnp.where(first_half, pltpu.roll(x, LANES - 16, 1), pltpu.roll(x, 16, 1))


def _norm_rope(x, seg_ones, gain, cos, sin):
    x2 = x * x
    hi = x2.astype(BF16)
    lo = (x2 - hi.astype(F32)).astype(BF16)
    ss = _dot(hi, seg_ones) + _dot(lo, seg_ones)
    xn = x * lax.rsqrt(ss * (1.0 / HEAD_DIM) + QK_EPS) * gain
    return xn * cos + _rope_partner(xn) * sin


def _proj_ab_kernel(x_ref, w_ref, cos_ref, sin_ref, ones_ref, qg_ref, kg_ref,
                    qkva_ref, qb_ref, kvb_ref):
    xb = x_ref[...].astype(BF16)
    ha = _dot(xb, w_ref[:, :3 * NA_WIDTH])
    qkva_ref[:, :NA_WIDTH] = (ha[:, :NA_WIDTH] * (ATTN_SCALE * LOG2E)).astype(BF16)
    qkva_ref[:, NA_WIDTH:] = ha[:, NA_WIDTH:].astype(BF16)
    hb = _dot(xb, w_ref[:, 3 * NA_WIDTH:])
    cos = cos_ref[...]
    sin = sin_ref[...]
    ones = ones_ref[...]
    for c in range(GQA_Q_WIDTH // LANES):
        sl = slice(c * LANES, (c + 1) * LANES)
        q = _norm_rope(hb[:, sl], ones, qg_ref[...], cos, sin)
        qb_ref[:, sl] = (q * (ATTN_SCALE * LOG2E)).astype(BF16)
    k0 = GQA_Q_WIDTH
    kvb_ref[:, :GQA_KV_WIDTH] = hb[:, k0 + GQA_KV_WIDTH:].astype(BF16)
    kvb_ref[:, GQA_KV_WIDTH:2 * GQA_KV_WIDTH] = jnp.ones((x_ref.shape[0], GQA_KV_WIDTH), BF16)
    kvb_ref[:, 2 * GQA_KV_WIDTH:] = _norm_rope(
        hb[:, k0:k0 + GQA_KV_WIDTH], ones, kg_ref[...], cos, sin).astype(BF16)


def _rope_tables(seq):
    t = jnp.arange(seq)
    row = (t // GRID_W).astype(F32)
    col = (t % GRID_W).astype(F32)
    quarter = HEAD_DIM // 4
    inv_freq = ROPE_THETA ** (-jnp.arange(quarter, dtype=F32) / quarter)
    ar = row[:, None] * inv_freq[None, :]
    ac = col[:, None] * inv_freq[None, :]
    cos = jnp.concatenate([jnp.cos(ar), jnp.cos(ar), jnp.cos(ac), jnp.cos(ac)], axis=-1)
    sin = jnp.concatenate([-jnp.sin(ar), jnp.sin(ar), -jnp.sin(ac), jnp.sin(ac)], axis=-1)
    reps = LANES // HEAD_DIM
    return jnp.tile(cos, (1, reps)), jnp.tile(sin, (1, reps))


def _proj_ab(x2d, seq, w_in, q_gain, k_gain):
    tokens = x2d.shape[0]
    tm = TOKEN_TILE
    tiles_per_seq = seq // tm
    cos, sin = _rope_tables(seq)
    seg = jnp.arange(LANES) // HEAD_DIM
    seg_ones = (seg[:, None] == seg[None, :]).astype(BF16)
    reps = LANES // HEAD_DIM
    qg = jnp.tile(q_gain, reps)[None, :]
    kg = jnp.tile(k_gain, reps)[None, :]
    tile = lambda width: pl.BlockSpec((tm, width), lambda i: (i, 0))
    table = pl.BlockSpec((tm, LANES), lambda i: (i % tiles_per_seq, 0))
    return pl.pallas_call(
        _proj_ab_kernel,
        grid=(tokens // tm,),
        in_specs=[tile(D_MODEL), _resident((D_MODEL, AB_IN)), table, table,
                  _resident((LANES, LANES)), _resident((1, LANES)), _resident((1, LANES))],
        out_specs=[tile(3 * NA_WIDTH), tile(GQA_Q_WIDTH), tile(3 * GQA_KV_WIDTH)],
        out_shape=[jax.ShapeDtypeStruct((tokens, 3 * NA_WIDTH), BF16),
                   jax.ShapeDtypeStruct((tokens, GQA_Q_WIDTH), BF16),
                   jax.ShapeDtypeStruct((tokens, 3 * GQA_KV_WIDTH), BF16)],
        compiler_params=_params(1),
        name="proj_ab",
    )(x2d, w_in, cos, sin, seg_ones, qg, kg)


def _na_bias_table(rpb):
    c = jnp.arange(GRID_W)
    cs = jnp.clip(c - NA_WIN_W // 2, 0, GRID_W - NA_WIN_W)
    col_valid = (c[None, :] >= cs[:, None]) & (c[None, :] < cs[:, None] + NA_WIN_W)
    dc = c[None, :] - c[:, None] + (NA_WIN_W - 1)
    toeplitz = jnp.zeros(rpb.shape[:2] + (GRID_W, GRID_W), F32)
    for j in range(2 * NA_WIN_W - 1):
        toeplitz = jnp.where(dc == j, rpb[:, :, j, None, None].astype(F32), toeplitz)
    toeplitz = jnp.where(col_valid, toeplitz * LOG2E, NEG_INF)
    bias = jnp.stack([toeplitz[:, NA_WIN_H - 1 - off:2 * NA_WIN_H - 1 - off]
                      for off in range(NA_WIN_H)])
    bias = jnp.transpose(bias, (0, 1, 3, 2, 4))
    return bias.reshape(NA_WIN_H, NA_HEADS // 2, 2 * GRID_W, NA_KEYS)


def _na_window_start(r, n_rows):
    return jnp.clip(r - NA_WIN_H // 2, 0, n_rows - NA_WIN_H)


def _na_kernel(q_ref, k_ref, v_ref, bias_ref, o_ref, *, n_rows):
    lower = lax.broadcasted_iota(jnp.int32, (GRID_W, LANES), 1) < HEAD_DIM
    for rr in range(NA_ROWS_PER_STEP):
        r = pl.program_id(1) * NA_ROWS_PER_STEP + rr
        first = _na_window_start(r, n_rows)
        start = pl.multiple_of(first * GRID_W, GRID_W)
        rows = slice(rr * GRID_W, (rr + 1) * GRID_W)
        for p in range(NA_HEADS // 2):
            sl = slice(p * LANES, (p + 1) * LANES)
            qp = q_ref[rows, sl]
            zero = jnp.zeros_like(qp)
            lhs = jnp.concatenate([jnp.where(lower, qp, zero), jnp.where(lower, zero, qp)], axis=0)
            k = k_ref[pl.ds(start, NA_KEYS), sl]
            v = v_ref[pl.ds(start, NA_KEYS), sl]
            s = _dot_nt(lhs, k) + bias_ref[r - first, p]
            e = jnp.exp2(s - jnp.max(s, axis=-1, keepdims=True))
            inv_l = 1.0 / jnp.sum(e, axis=-1, keepdims=True)
            o = _dot(e.astype(BF16), v) * inv_l
            o_ref[rows, sl] = jnp.where(lower, o[:GRID_W], o[GRID_W:]).astype(BF16)


def _na_attn(qkva, batch, seq, bias_table):
    n_rows = seq // GRID_W
    assert n_rows >= NA_WIN_H and n_rows % NA_ROWS_PER_STEP == 0
    n_steps = n_rows // NA_ROWS_PER_STEP
    tq = NA_ROWS_PER_STEP * GRID_W
    return pl.pallas_call(
        functools.partial(_na_kernel, n_rows=n_rows),
        grid=(batch, n_steps),
        in_specs=[
            pl.BlockSpec((tq, NA_WIDTH), lambda b, i: (b * n_steps + i, 0)),
            pl.BlockSpec((seq, NA_WIDTH), lambda b, i: (b, 1)),
            pl.BlockSpec((seq, NA_WIDTH), lambda b, i: (b, 2)),
            _resident(bias_table.shape),
        ],
        out_specs=pl.BlockSpec((tq, NA_WIDTH), lambda b, i: (b * n_steps + i, 0)),
        out_shape=jax.ShapeDtypeStruct((batch * seq, NA_WIDTH), BF16),
        compiler_params=_params(2),
        name="na_attn",
    )(qkva, qkva, qkva, bias_table)


def _gqa_kernel(q_ref, k_ref, v_ref, o_ref, *, n_chunks):
    tq = q_ref.shape[0] // n_chunks
    lower = lax.broadcasted_iota(jnp.int32, (tq, LANES), 1) < HEAD_DIM
    k = k_ref[...]
    v = v_ref[...]
    heads_per_kv = GQA_HEADS // GQA_KV_HEADS

    def head(h, rows):
        kv_head = h // heads_per_kv
        swap = (h % 2) != kv_head
        qp = q_ref[rows, (h // 2) * LANES:(h // 2 + 1) * LANES]
        if swap:
            qp = pltpu.roll(qp, HEAD_DIM, 1)
        keep = lower if kv_head == 0 else jnp.logical_not(lower)
        lhs = jnp.where(keep, qp, jnp.zeros_like(qp))
        s = _dot_nt(lhs, k)
        e = jnp.exp2(s - jnp.max(s, axis=-1, keepdims=True)).astype(BF16)
        ov = _dot(e, v)
        o = ov[:, :LANES] * (1.0 / ov[:, LANES:])
        return pltpu.roll(o, HEAD_DIM, 1) if swap else o

    for c in range(n_chunks):
        rows = slice(c * tq, (c + 1) * tq)
        for p in range(GQA_HEADS // 2):
            out = jnp.where(lower, head(2 * p, rows), head(2 * p + 1, rows))
            o_ref[rows, p * LANES:(p + 1) * LANES] = out.astype(BF16)


def _gqa_attn(qb, kvb, batch, seq):
    chunk = min(seq, GQA_SCORE_ELEMS // seq)
    n_chunks = min(GQA_ROW_CHUNKS, seq // chunk)
    tq = chunk * n_chunks
    n_q = seq // tq
    return pl.pallas_call(
        functools.partial(_gqa_kernel, n_chunks=n_chunks),
        grid=(batch, n_q),
        in_specs=[
            pl.BlockSpec((tq, GQA_Q_WIDTH), lambda b, i: (b * n_q + i, 0)),
            pl.BlockSpec((seq, GQA_KV_WIDTH), lambda b, i: (b, 2)),
            pl.BlockSpec((seq, 2 * GQA_KV_WIDTH), lambda b, i: (b, 0)),
        ],
        out_specs=pl.BlockSpec((tq, GQA_Q_WIDTH), lambda b, i: (b * n_q + i, 0)),
        out_shape=jax.ShapeDtypeStruct((batch * seq, GQA_Q_WIDTH), BF16),
        compiler_params=_params(2),
        name="gqa_attn",
    )(qb, kvb, kvb)


def _xattn_block(x, kt_ref, v_ref, wq_ref, wo_ref, g_ref, b_ref):
    q = (_dot(x.astype(BF16), wq_ref[...]) * (XA_SCALE * LOG2E)).astype(BF16)
    outs = []
    for h in range(XA_HEADS):
        sl = slice(h * XA_HEAD_DIM, (h + 1) * XA_HEAD_DIM)
        s = _dot(q[:, sl], kt_ref[sl, :])
        e = jnp.exp2(s - jnp.max(s, axis=-1, keepdims=True))
        inv_l = 1.0 / jnp.sum(e, axis=-1, keepdims=True)
        outs.append((_dot(e.astype(BF16), v_ref[:, sl]) * inv_l).astype(BF16))
    m = _dot(jnp.concatenate(outs, axis=-1), wo_ref[...])
    return _layer_norm(DN_ALPHA * x + m, g_ref[...], b_ref[...])


def _xattn_specs(n_mem):
    return [pl.BlockSpec((None, D_MODEL, n_mem), lambda bb, i: (bb, 0, 0)),
            pl.BlockSpec((None, n_mem, D_MODEL), lambda bb, i: (bb, 0, 0)),
            _resident((D_MODEL, D_MODEL)), _resident((D_MODEL, D_MODEL)),
            _resident((1, D_MODEL)), _resident((1, D_MODEL))]


def _out_xattn_kernel(x_ref, oa_ref, ob_ref, w_ref, g_ref, b_ref, *rest):
    *xattn_refs, y_ref = rest
    m = _dot(oa_ref[...], w_ref[:NA_WIDTH, :]) + _dot(ob_ref[...], w_ref[NA_WIDTH:, :])
    x1 = _layer_norm(DN_ALPHA * x_ref[...] + m, g_ref[...], b_ref[...])
    y_ref[...] = _xattn_block(x1, *xattn_refs)


def _out_xattn(x2d, oa, ob, batch, seq, w_out, g, b, xattn_operands):
    tm = TOKEN_TILE
    n_t = seq // tm
    n_mem = xattn_operands[1].shape[1]
    tile = lambda width: pl.BlockSpec((tm, width), lambda bb, i: (bb * n_t + i, 0))
    return pl.pallas_call(
        _out_xattn_kernel,
        grid=(batch, n_t),
        in_specs=[tile(D_MODEL), tile(NA_WIDTH), tile(GQA_Q_WIDTH),
                  _resident((NA_WIDTH + GQA_Q_WIDTH, D_MODEL)),
                  _resident((1, D_MODEL)), _resident((1, D_MODEL))] + _xattn_specs(n_mem),
        out_specs=tile(D_MODEL),
        out_shape=jax.ShapeDtypeStruct((batch * seq, D_MODEL), F32),
        compiler_params=_params(2),
        name="out_xattn",
    )(x2d, oa, ob, w_out, g[None, :], b[None, :], *xattn_operands)


POOL_HALO = F32_ROWS


def _pool_xattn_kernel(xp_ref, x_ref, xn_ref, w_ref, pb_ref, ps_ref, g_ref, b_ref, *rest, seq):
    *xattn_refs, y_ref, xe_ref = rest
    i = pl.program_id(1)
    tm = x_ref.shape[0]
    x = x_ref[...]
    xe_ref[:POOL_HALO, :] = jnp.where(i > 0, xp_ref[...], 0.0)
    xe_ref[POOL_HALO:POOL_HALO + tm, :] = x
    xe_ref[POOL_HALO + tm:, :] = jnp.where(i < pl.num_programs(1) - 1, xn_ref[...], 0.0)
    pos = i * tm + lax.broadcasted_iota(jnp.int32, (tm, 1), 0)
    parts = []
    for gi, win in enumerate(POOL_WINDOWS):
        cols = slice(gi * POOL_CH, (gi + 1) * POOL_CH)
        lo, hi = -(win // 2), win - 1 - win // 2
        total = xe_ref[POOL_HALO + lo:POOL_HALO + lo + tm, cols]
        for d in range(lo + 1, hi + 1):
            total = total + xe_ref[POOL_HALO + d:POOL_HALO + d + tm, cols]
        cnt = jnp.minimum(pos + hi, seq - 1) - jnp.maximum(pos + lo, 0) + 1
        xi = x[:, cols]
        pooled = (total / cnt.astype(F32) - xi).astype(BF16)
        y = (_dot(pooled, w_ref[gi]) + pb_ref[:, cols]) * ps_ref[:, cols]
        parts.append(DN_ALPHA * xi + y)
    x1 = _layer_norm(jnp.concatenate(parts, axis=-1), g_ref[...], b_ref[...])
    y_ref[...] = _xattn_block(x1, *xattn_refs)


def _pool_xattn(x2d, batch, seq, w, pb, ps, g, b, xattn_operands):
    tm = TOKEN_TILE
    n_t = seq // tm
    n_mem = xattn_operands[1].shape[1]
    halo_per_tile = tm // POOL_HALO
    n_halo = batch * seq // POOL_HALO
    prev_idx = lambda bb, i: (jnp.maximum((bb * n_t + i) * halo_per_tile - 1, 0), 0)
    next_idx = lambda bb, i: (jnp.minimum((bb * n_t + i + 1) * halo_per_tile, n_halo - 1), 0)
    return pl.pallas_call(
        functools.partial(_pool_xattn_kernel, seq=seq),
        grid=(batch, n_t),
        in_specs=[
            pl.BlockSpec((POOL_HALO, D_MODEL), prev_idx),
            pl.BlockSpec((tm, D_MODEL), lambda bb, i: (bb * n_t + i, 0)),
            pl.BlockSpec((POOL_HALO, D_MODEL), next_idx),
            _resident((POOL_GROUPS, POOL_CH, POOL_CH)),
            _resident((1, D_MODEL)), _resident((1, D_MODEL)),
            _resident((1, D_MODEL)), _resident((1, D_MODEL)),
        ] + _xattn_specs(n_mem),
        out_specs=pl.BlockSpec((tm, D_MODEL), lambda bb, i: (bb * n_t + i, 0)),
        out_shape=jax.ShapeDtypeStruct((batch * seq, D_MODEL), F32),
        scratch_shapes=[pltpu.VMEM((tm + 2 * POOL_HALO, D_MODEL), F32)],
        compiler_params=_params(2),
        name="pool_xattn",
    )(x2d, x2d, x2d, w, pb.reshape(1, D_MODEL), ps[None, :], g[None, :], b[None, :],
      *xattn_operands)


def _kv_proj_kernel(mem_ref, w_ref, kt_ref, v_ref):
    kv = _dot(mem_ref[...].astype(BF16), w_ref[...])
    kt_ref[...] = kv[:, :D_MODEL].T.astype(BF16)
    v_ref[...] = kv[:, D_MODEL:].astype(BF16)


def _kv_proj(mem2d, batch, n_mem, wkv):
    return pl.pallas_call(
        _kv_proj_kernel,
        grid=(batch,),
        in_specs=[pl.BlockSpec((n_mem, D_MODEL), lambda b: (b, 0)),
                  _resident((D_MODEL, 2 * D_MODEL))],
        out_specs=[pl.BlockSpec((None, D_MODEL, n_mem), lambda b: (b, 0, 0)),
                   pl.BlockSpec((None, n_mem, D_MODEL), lambda b: (b, 0, 0))],
        out_shape=[jax.ShapeDtypeStruct((batch, D_MODEL, n_mem), BF16),
                   jax.ShapeDtypeStruct((batch, n_mem, D_MODEL), BF16)],
        compiler_params=_params(1),
        name="kv_proj",
    )(mem2d, wkv)


FFN_HALO = BF16_ROWS


def _gelu_tanh(x):
    c = 0.7978845608028654
    return 0.5 * x * (1.0 + jnp.tanh(c * (x + 0.044715 * (x * x * x))))


def _ffn_ln_kernel(xp_ref, x_ref, xn_ref, wup_ref, cw_ref, cb_ref, wdn_ref, g_ref, b_ref, y_ref,
                   xe_ref, act_ref):
    i = pl.program_id(1)
    tm = x_ref.shape[0]
    x = x_ref[...]
    xe_ref[:FFN_HALO, :] = jnp.where(i > 0, xp_ref[...], 0.0).astype(BF16)
    xe_ref[FFN_HALO:FFN_HALO + tm, :] = x.astype(BF16)
    xe_ref[FFN_HALO + tm:, :] = jnp.where(i < pl.num_programs(1) - 1, xn_ref[...], 0.0).astype(BF16)
    for c in range(FFN_NCHUNK):
        h = _dot(xe_ref[...], wup_ref[c])
        rows = h.shape[0]
        mid = slice(FFN_HALO, FFN_HALO + tm)
        cw = cw_ref[c]
        hc = (pltpu.roll(h, 1, 0)[mid] * cw[0:1]
              + h[mid] * cw[1:2]
              + pltpu.roll(h, rows - 1, 0)[mid] * cw[2:3]
              + cb_ref[c])
        act = _gelu_tanh(hc[:, FFN_CHUNK:]) * hc[:, :FFN_CHUNK]
        act_ref[:, c * FFN_CHUNK:(c + 1) * FFN_CHUNK] = act.astype(BF16)
    m = _dot(act_ref[...], wdn_ref[...])
    y_ref[...] = _layer_norm(DN_ALPHA * x + m, g_ref[...], b_ref[...])


def _ffn_weights(w_up, conv_w, conv_b):
    def pair(a):
        lead = a.shape[:-1]
        a = a.reshape(lead + (2, FFN_NCHUNK, FFN_CHUNK))
        a = jnp.moveaxis(a, -3, -2)
        a = a.reshape(lead + (FFN_NCHUNK, 2 * FFN_CHUNK))
        return jnp.moveaxis(a, -2, 0)
    return pair(w_up).astype(BF16), pair(conv_w), pair(conv_b[None, :])


def _ffn_ln(x2d, batch, seq, wup, cw, cb, wdn, g, b):
    tm = min(seq, FFN_TOKEN_TILE)
    n_t = seq // tm
    halo_per_tile = tm // FFN_HALO
    n_halo = batch * seq // FFN_HALO
    prev_idx = lambda bb, i: (jnp.maximum((bb * n_t + i) * halo_per_tile - 1, 0), 0)
    next_idx = lambda bb, i: (jnp.minimum((bb * n_t + i + 1) * halo_per_tile, n_halo - 1), 0)
    return pl.pallas_call(
        _ffn_ln_kernel,
        grid=(batch, n_t),
        in_specs=[
            pl.BlockSpec((FFN_HALO, D_MODEL), prev_idx),
            pl.BlockSpec((tm, D_MODEL), lambda bb, i: (bb * n_t + i, 0)),
            pl.BlockSpec((FFN_HALO, D_MODEL), next_idx),
            _resident((FFN_NCHUNK, D_MODEL, 2 * FFN_CHUNK)),
            _resident((FFN_NCHUNK, 3, 2 * FFN_CHUNK)),
            _resident((FFN_NCHUNK, 1, 2 * FFN_CHUNK)),
            _resident((D_FF, D_MODEL)),
            _resident((1, D_MODEL)), _resident((1, D_MODEL)),
        ],
        out_specs=pl.BlockSpec((tm, D_MODEL), lambda bb, i: (bb * n_t + i, 0)),
        out_shape=jax.ShapeDtypeStruct((batch * seq, D_MODEL), F32),
        scratch_shapes=[pltpu.VMEM((tm + 2 * FFN_HALO, D_MODEL), BF16),
                        pltpu.VMEM((tm, D_FF), BF16)],
        compiler_params=_params(2),
        name="ffn_ln",
    )(x2d, x2d, x2d, wup, cw, cb, wdn, g[None, :], b[None, :])


def _trunk(x, mem, p):
    batch, seq, _ = x.shape
    n_mem = mem.shape[1]
    x2d = x.reshape(batch * seq, D_MODEL)
    mem2d = mem.reshape(batch * n_mem, D_MODEL)
    for l in range(DEPTH):
        i = l // 2
        kt, v = _kv_proj(mem2d, batch, n_mem, p["xa_wkv"][l])
        xattn_operands = (kt, v, p["xa_wq"][l], p["xa_wo"][l],
                          p["ln2_g"][l][None, :], p["ln2_b"][l][None, :])
        if l % 2 == 0:
            qkva, qb, kvb = _proj_ab(x2d, seq, p["ab_w_in"][i], p["gqa_q_gain"][i], p["gqa_k_gain"][i])
            oa = _na_attn(qkva, batch, seq, p["na_bias"][i])
            ob = _gqa_attn(qb, kvb, batch, seq)
            x2d = _out_xattn(x2d, oa, ob, batch, seq, p["ab_w_out"][i], p["ln1_g"][l], p["ln1_b"][l],
                             xattn_operands)
        else:
            x2d = _pool_xattn(x2d, batch, seq, p["pool_w"][i], p["pool_b"][i], p["pool_scale"][i],
                              p["ln1_g"][l], p["ln1_b"][l], xattn_operands)
        wup, cw, cb = p["ffn"][l]
        x2d = _ffn_ln(x2d, batch, seq, wup, cw, cb, p["ffn_w_down"][l], p["ln3_g"][l], p["ln3_b"][l])
    return x2d.reshape(batch, seq, D_MODEL)


def kernel(x_prompt, x_sample, mem_prompt, mem_sample, ab_w_in, na_rpb, gqa_q_gain, gqa_k_gain,
           ab_w_out, pool_w, pool_b, pool_scale, ln1_g, ln1_b, xa_wq, xa_wkv, xa_wo, ln2_g, ln2_b,
           ffn_w_up, ffn_conv_w, ffn_conv_b, ffn_w_down, ln3_g, ln3_b):
    p = dict(
        ab_w_in=ab_w_in.astype(BF16), gqa_q_gain=gqa_q_gain, gqa_k_gain=gqa_k_gain,
        na_bias=[_na_bias_table(na_rpb[i]) for i in range(na_rpb.shape[0])],
        ab_w_out=ab_w_out.astype(BF16),
        pool_w=pool_w.astype(BF16), pool_b=pool_b, pool_scale=pool_scale,
        ln1_g=ln1_g, ln1_b=ln1_b, ln2_g=ln2_g, ln2_b=ln2_b, ln3_g=ln3_g, ln3_b=ln3_b,
        xa_wq=xa_wq.astype(BF16), xa_wkv=xa_wkv.astype(BF16), xa_wo=xa_wo.astype(BF16),
        ffn=[_ffn_weights(ffn_w_up[l], ffn_conv_w[l], ffn_conv_b[l]) for l in range(DEPTH)],
        ffn_w_down=ffn_w_down.astype(BF16),
    )
    return (_trunk(x_prompt, mem_prompt, p), _trunk(x_sample, mem_sample, p))
```
